```python
import math, functools
import jax, jax.numpy as jnp
from jax import lax
import numpy as np

D_MODEL = 2048
BATCH = 4
SEQ = 4096
DEPTH = 2
DEC_BATCH = 128
DEC_SEQ = 1
PAST_LEN = 16384
PAGE_SIZE = 128

M_HEADS = 4
M_QK = 256
M_V = D_MODEL // M_HEADS
M_CHUNK = 64
A_HEADS = 16
Q_LORA = 512
KV_LORA = 512
NOPE_DIM = 128
ROPE_DIM = 64
QK_HEAD = NOPE_DIM + ROPE_DIM
A_V = D_MODEL // A_HEADS
ROPE_THETA = 10000.0
Q_BLOCK = 128
D_FF = 5504
PLE_DIM = 256
EPS = 1e-6
IN_WIDTHS = (M_HEADS * M_QK, M_HEADS * M_QK, M_HEADS * M_V, D_MODEL, M_HEADS, M_HEADS, Q_LORA, KV_LORA, ROPE_DIM)
IN_WIDTH = sum(IN_WIDTHS)

kernel_name = 'hybrid_mlstm_mla_macaron_decoder_step'


def rmsnorm(x, g):
    xf = x.astype(jnp.float32)
    y = xf * lax.rsqrt(jnp.mean(xf * xf, axis=-1, keepdims=True) + EPS)
    return (y * g.astype(jnp.float32)).astype(x.dtype)


def swiglu(x, w_up, w_down):
    a, b = jnp.split(x @ w_up, 2, axis=-1)
    return (jax.nn.silu(a) * b) @ w_down


def split_cols(z, widths):
    outs, start = [], 0
    for w in widths:
        outs.append(z[..., start:start + w])
        start += w
    return outs


def rope(x, pos):
    half = ROPE_DIM // 2
    inv = ROPE_THETA ** (-jnp.arange(half, dtype=jnp.float32) / half)
    ang = pos.astype(jnp.float32)[:, None] * inv[None, :]
    cos = jnp.cos(ang)[:, None, :]
    sin = jnp.sin(ang)[:, None, :]
    xf = x.astype(jnp.float32)
    x1, x2 = xf[..., :half], xf[..., half:]
    return jnp.concatenate([x1 * cos - x2 * sin, x2 * cos + x1 * sin], axis=-1).astype(x.dtype)


def mlstm_chunkwise(q, k, v, i_pre, logf, C0, n0, m0):
    f32 = jnp.float32
    B, S, H, _ = q.shape
    L = math.gcd(S, M_CHUNK)
    nc = S // L

    def chunk(a):
        a = a.astype(f32).reshape((B, nc, L) + a.shape[2:])
        return jnp.swapaxes(jnp.moveaxis(a, 1, 0), 2, 3)

    causal = jnp.tril(jnp.ones((L, L), dtype=bool))

    def step(carry, xs):
        C, n, m = carry
        qc, kc, vc, ic, fc = xs
        b = jnp.cumsum(fc, axis=-1)
        d = jnp.where(causal, b[..., :, None] - b[..., None, :] + ic[..., None, :], -jnp.inf)
        inter = b + m[..., None]
        m_t = jnp.maximum(inter, jnp.max(d, axis=-1))
        w_inter = jnp.exp(inter - m_t)
        s = jnp.einsum('bhtk,bhjk->bhtj', qc, kc) * jnp.exp(d - m_t[..., None])
        num = w_inter[..., None] * jnp.einsum('bhtk,bhkv->bhtv', qc, C) + jnp.einsum('bhtj,bhjv->bhtv', s, vc)
        den = w_inter * jnp.einsum('bhtk,bhk->bht', qc, n) + jnp.sum(s, axis=-1)
        h = num / jnp.maximum(jnp.abs(den), jnp.exp(-m_t))[..., None]
        g = b[..., -1:] - b + ic
        m_new = jnp.maximum(b[..., -1] + m, jnp.max(g, axis=-1))
        a_prev = jnp.exp(b[..., -1] + m - m_new)
        a_j = jnp.exp(g - m_new[..., None])
        C_new = a_prev[..., None, None] * C + jnp.einsum('bhj,bhjk,bhjv->bhkv', a_j, kc, vc)
        n_new = a_prev[..., None] * n + jnp.einsum('bhj,bhjk->bhk', a_j, kc)
        return (C_new, n_new, m_new), h

    carry0 = (C0.astype(f32), n0.astype(f32), m0.astype(f32))
    (C, n, m), h = lax.scan(step, carry0, (chunk(q), chunk(k), chunk(v), chunk(i_pre), chunk(logf)))
    h = jnp.moveaxis(jnp.swapaxes(h, 2, 3), 0, 1).reshape(B, S, H, v.shape[-1])
    return h, C, n, m


def latent_keys_values(c, kr, w_ukv, g_k):
    kv = jnp.einsum('...c,chd->...hd', c, w_ukv)
    k_nope, v = kv[..., :NOPE_DIM], kv[..., NOPE_DIM:]
    k_pe = jnp.broadcast_to(kr[..., None, :], k_nope.shape[:-1] + (ROPE_DIM,))
    k = rmsnorm(jnp.concatenate([k_nope, k_pe], axis=-1), g_k)
    return k, v


def causal_block_attention(q, k, v):
    B, S, H, Dq = q.shape
    nb = S // Q_BLOCK
    scale = QK_HEAD ** -0.5
    kpos = jnp.arange(S)
    qb = jnp.moveaxis(q.reshape(B, nb, Q_BLOCK, H, Dq), 1, 0)

    def block(args):
        qi, bi = args
        s = jnp.einsum('bqhd,bkhd->bhqk', qi, k).astype(jnp.float32) * scale
        qpos = bi * Q_BLOCK + jnp.arange(Q_BLOCK)
        s = jnp.where(kpos[None, :] <= qpos[:, None], s, -jnp.inf)
        p = jax.nn.softmax(s, axis=-1).astype(v.dtype)
        return jnp.einsum('bhqk,bkhd->bqhd', p, v)

    o = lax.map(block, (qb, jnp.arange(nb)))
    return jnp.moveaxis(o, 0, 1).reshape(B, S, H, v.shape[-1])


def prompt_attend(q, ckv, kpe, w_ukv, g_k):
    k, v = latent_keys_values(ckv, kpe, w_ukv, g_k)
    return causal_block_attention(q, k, v)


def paged_latent_attention(q, ckv_new, kpe_new, ckv_pool, kpe_pool, page_table, w_ukv, g_k):
    T = q.shape[1]
    past = page_table.shape[1] * ckv_pool.shape[1]
    scale = QK_HEAD ** -0.5
    mask = jnp.arange(past + T)[None, :] <= (past + jnp.arange(T))[:, None]

    def one(args):
        pt, qs, cn, kn = args
        c = jnp.concatenate([ckv_pool[pt].reshape(past, KV_LORA), cn.astype(ckv_pool.dtype)], axis=0)
        kr = jnp.concatenate([kpe_pool[pt].reshape(past, ROPE_DIM), kn.astype(kpe_pool.dtype)], axis=0)
        k, v = latent_keys_values(c, kr, w_ukv, g_k)
        s = jnp.einsum('qhd,khd->hqk', qs, k).astype(jnp.float32) * scale
        s = jnp.where(mask[None], s, -jnp.inf)
        p = jax.nn.softmax(s, axis=-1).astype(v.dtype)
        return jnp.einsum('hqk,khd->qhd', p, v)

    return lax.map(one, (page_table, q, ckv_new, kpe_new))


def run_layer(x, p, pos, C0, n0, m0, attend, prm):
    f32 = jnp.float32
    g = prm['norm_g']
    B, S, _ = x.shape
    x = x + 0.5 * swiglu(rmsnorm(x, g[0]), prm['w_up'][0], prm['w_down'][0])
    u = rmsnorm(x, g[1])
    z = u @ prm['w_in']
    zq, zk, zv, zo, zi, zf, zcq, zckv, zkr = split_cols(z, IN_WIDTHS)
    mq = zq.reshape(B, S, M_HEADS, M_QK)
    mk = zk.reshape(B, S, M_HEADS, M_QK) * (M_QK ** -0.5)
    mv = zv.reshape(B, S, M_HEADS, M_V)
    i_pre = zi.astype(f32) + prm['b_if'][0].astype(f32)
    logf = jax.nn.log_sigmoid(zf.astype(f32) + prm['b_if'][1].astype(f32))
    h, C, n, m = mlstm_chunkwise(mq, mk, mv, i_pre, logf, C0, n0, m0)
    y_m = jax.nn.sigmoid(zo) * rmsnorm(h.astype(x.dtype), prm['g_mh']).reshape(B, S, D_MODEL)
    cq = rmsnorm(zcq, prm['g_cq'])
    q = jnp.einsum('bsc,chd->bshd', cq, prm['w_uq'])
    q = rmsnorm(jnp.concatenate([q[..., :NOPE_DIM], rope(q[..., NOPE_DIM:], pos)], axis=-1), prm['g_q'])
    ckv = rmsnorm(zckv, prm['g_ckv'])
    kpe = rope(zkr[:, :, None, :], pos)[:, :, 0, :]
    y_a = attend(q, ckv, kpe).reshape(B, S, D_MODEL)
    gates = jax.nn.sigmoid(u @ prm['w_gate'] + prm['b_gate'])
    x = x + (gates[..., :D_MODEL] * y_m + gates[..., D_MODEL:] * y_a) @ prm['w_out']
    x = x + 0.5 * swiglu(rmsnorm(x, g[2]), prm['w_up'][1], prm['w_down'][1])
    x = x + jax.nn.sigmoid(rmsnorm(x, g[3]) @ prm['w_pg']) * (p @ prm['w_ple'])
    return x, ckv, kpe, C, n, m


def setup_inputs(seed: int = 0) -> dict:
    key = jax.random.key(seed)
    ks = jax.random.split(key, 32)
    f32 = jnp.float32

    def nrm(k, shape, scale):
        return scale * jax.random.normal(k, shape, f32)

    n_pages = PAST_LEN // PAGE_SIZE
    n_used = DEC_BATCH * n_pages
    n_pool = n_used + n_used // 4
    page_table = jax.random.permutation(ks[0], n_pool)[:n_used].reshape(DEC_BATCH, n_pages).astype(jnp.int32)
    return {
        'x_prompt': nrm(ks[1], (BATCH, SEQ, D_MODEL), 1.0),
        'x_sample': nrm(ks[2], (DEC_BATCH, DEC_SEQ, D_MODEL), 1.0),
        'cache_ckv': nrm(ks[3], (DEPTH, n_pool, PAGE_SIZE, KV_LORA), 1.0),
        'cache_kpe': nrm(ks[4], (DEPTH, n_pool, PAGE_SIZE, ROPE_DIM), 1.0),
        'state_C': nrm(ks[5], (DEPTH, DEC_BATCH, M_HEADS, M_QK, M_V), 0.1),
        'state_n': nrm(ks[6], (DEPTH, DEC_BATCH, M_HEADS, M_QK), 0.1),
        'state_m': nrm(ks[7], (DEPTH, DEC_BATCH, M_HEADS), 0.5),
        'page_table': page_table,
        'p_prompt': nrm(ks[8], (DEPTH, BATCH, SEQ, PLE_DIM), 1.0),
        'p_sample': nrm(ks[9], (DEPTH, DEC_BATCH, DEC_SEQ, PLE_DIM), 1.0),
        'norm_g': 1.0 + nrm(ks[10], (DEPTH, 4, D_MODEL), 0.05),
        'w_ffn_up': nrm(ks[11], (DEPTH, 2, D_MODEL, 2 * D_FF), D_MODEL ** -0.5),
        'w_ffn_down': nrm(ks[12], (DEPTH, 2, D_FF, D_MODEL), D_FF ** -0.5),
        'w_in': nrm(ks[13], (DEPTH, D_MODEL, IN_WIDTH), D_MODEL ** -0.5),
        'b_if': jnp.stack([nrm(ks[14], (DEPTH, M_HEADS), 0.1), 3.0 + nrm(ks[15], (DEPTH, M_HEADS), 0.1)], axis=1),
        'g_mh': 1.0 + nrm(ks[16], (DEPTH, M_HEADS, M_V), 0.05),
        'g_cq': 1.0 + nrm(ks[17], (DEPTH, Q_LORA), 0.05),
        'w_uq': nrm(ks[18], (DEPTH, Q_LORA, A_HEADS, QK_HEAD), Q_LORA ** -0.5),
        'g_ckv': 1.0 + nrm(ks[19], (DEPTH, KV_LORA), 0.05),
        'w_ukv': nrm(ks[20], (DEPTH, KV_LORA, A_HEADS, NOPE_DIM + A_V), KV_LORA ** -0.5),
        'g_q': 1.0 + nrm(ks[21], (DEPTH, QK_HEAD), 0.05),
        'g_k': 1.0 + nrm(ks[22], (DEPTH, QK_HEAD), 0.05),
        'w_gate': nrm(ks[23], (DEPTH, D_MODEL, 2 * D_MODEL), D_MODEL ** -0.5),
        'b_gate': nrm(ks[24], (DEPTH, 2 * D_MODEL), 0.1),
        'w_out': nrm(ks[25], (DEPTH, D_MODEL, D_MODEL), D_MODEL ** -0.5),
        'w_ple': nrm(ks[26], (DEPTH, PLE_DIM, D_MODEL), PLE_DIM ** -0.5),
        'w_pg': nrm(ks[27], (DEPTH, D_MODEL, D_MODEL), D_MODEL ** -0.5),
    }


def reference(x_prompt, x_sample, cache_ckv, cache_kpe, state_C, state_n, state_m, page_table,
              p_prompt, p_sample, norm_g, w_ffn_up, w_ffn_down, w_in, b_if, g_mh, g_cq, w_uq,
              g_ckv, w_ukv, g_q, g_k, w_gate, b_gate, w_out, w_ple, w_pg):
    f32 = jnp.float32
    B, S, _ = x_prompt.shape
    T = x_sample.shape[1]
    past = page_table.shape[1] * cache_ckv.shape[2]
    pos_p = jnp.arange(S)
    pos_s = past + jnp.arange(T)
    C0 = jnp.zeros((B, M_HEADS, M_QK, M_V), f32)
    n0 = jnp.zeros((B, M_HEADS, M_QK), f32)
    m0 = jnp.zeros((B, M_HEADS), f32)
    xp, xs = x_prompt, x_sample
    outs_p, outs_s = [], []
    for l in range(DEPTH):
        prm = {'norm_g': norm_g[l], 'w_up': w_ffn_up[l], 'w_down': w_ffn_down[l], 'w_in': w_in[l],
               'b_if': b_if[l], 'g_mh': g_mh[l], 'g_cq': g_cq[l], 'w_uq': w_uq[l], 'g_ckv': g_ckv[l],
               'g_q': g_q[l], 'w_gate': w_gate[l], 'b_gate': b_gate[l], 'w_out': w_out[l],
               'w_ple': w_ple[l], 'w_pg': w_pg[l]}
        attend_p = functools.partial(prompt_attend, w_ukv=w_ukv[l], g_k=g_k[l])
        attend_s = functools.partial(paged_latent_attention, ckv_pool=cache_ckv[l], kpe_pool=cache_kpe[l],
                                     page_table=page_table, w_ukv=w_ukv[l], g_k=g_k[l])
        xp, ckv_p, kpe_p, C_p, n_p, m_p = run_layer(xp, p_prompt[l], pos_p, C0, n0, m0, attend_p, prm)
        xs, ckv_s, kpe_s, C_s, n_s, m_s = run_layer(xs, p_sample[l], pos_s, state_C[l], state_n[l],
                                                    state_m[l], attend_s, prm)
        outs_p.append((ckv_p, kpe_p, C_p, n_p, m_p))
        outs_s.append((ckv_s, kpe_s, C_s, n_s, m_s))
    ckv_prompt = jnp.stack([o[0] for o in outs_p])
    kpe_prompt = jnp.stack([o[1] for o in outs_p])
    C_prompt = jnp.stack([o[2] for o in outs_p])
    n_prompt = jnp.stack([o[3] for o in outs_p])
    m_prompt = jnp.stack([o[4] for o in outs_p])
    ckv_sample = jnp.stack([o[0] for o in outs_s])
    kpe_sample = jnp.stack([o[1] for o in outs_s])
    C_sample = jnp.stack([o[2] for o in outs_s])
    n_sample = jnp.stack([o[3] for o in outs_s])
    m_sample = jnp.stack([o[4] for o in outs_s])
    return (xp, xs, ckv_prompt, kpe_prompt, C_prompt, n_prompt, m_prompt,
            ckv_sample, kpe_sample, C_sample, n_sample, m_sample)
```

```python
import functools
import math

import jax
import jax.numpy as jnp
from jax import lax
from jax.experimental import pallas as pl
from jax.experimental.pallas import tpu as pltpu

F32 = jnp.float32
BF16 = jnp.bfloat16
EPS = 1e-6
ROPE_THETA = 10000.0
LANE = 128
VMEM_LIMIT_BYTES = 56 * 2**20
_NT = (((1,), (1,)), ((), ()))


def _cparams(*sem):
    return pltpu.CompilerParams(dimension_semantics=sem, vmem_limit_bytes=VMEM_LIMIT_BYTES)


def _rms(x, g):
    return x * lax.rsqrt(jnp.mean(x * x, axis=-1, keepdims=True) + EPS) * g


def _log_sigmoid(x):
    return jnp.minimum(x, 0.0) - jnp.log(1.0 + jnp.exp(-jnp.abs(x)))


def _bdot(a, b):
    return jnp.dot(a, b, preferred_element_type=F32)


def _tile(n, pref):
    if n <= pref:
        return n
    t = pref
    while n % t:
        t -= 8
    return t


def _ffn_body(x_ref, g_ref, wa_ref, wb_ref, wd_ref, o_ref, xn_ref, acc_ref):
    j = pl.program_id(1)

    @pl.when(j == 0)
    def _():
        xn_ref[...] = _rms(x_ref[...], g_ref[...]).astype(BF16)
        acc_ref[...] = jnp.zeros_like(acc_ref)

    xn = xn_ref[...]
    a = _bdot(xn, wa_ref[...])
    b = _bdot(xn, wb_ref[...])
    h = (a * jax.nn.sigmoid(a) * b).astype(BF16)
    acc_ref[...] += _bdot(h, wd_ref[...])

    @pl.when(j == pl.num_programs(1) - 1)
    def _():
        o_ref[...] = x_ref[...] + 0.5 * acc_ref[...]


def ffn(x, g, wa, wb, wd, *, tm_pref=512, tf_pref=512):
    m, d = x.shape
    fp = wa.shape[1]
    tm, tf = _tile(m, tm_pref), _tile(fp, tf_pref)
    return pl.pallas_call(
        _ffn_body,
        grid=(m // tm, fp // tf),
        in_specs=[
            pl.BlockSpec((tm, d), lambda i, j: (i, 0)),
            pl.BlockSpec((1, d), lambda i, j: (0, 0)),
            pl.BlockSpec((d, tf), lambda i, j: (0, j)),
            pl.BlockSpec((d, tf), lambda i, j: (0, j)),
            pl.BlockSpec((tf, d), lambda i, j: (j, 0)),
        ],
        out_specs=pl.BlockSpec((tm, d), lambda i, j: (i, 0)),
        out_shape=jax.ShapeDtypeStruct((m, d), F32),
        scratch_shapes=[pltpu.VMEM((tm, d), BF16), pltpu.VMEM((tm, d), F32)],
        compiler_params=_cparams("parallel", "arbitrary"),
        name="ffn",
    )(x, g, wa, wb, wd)


def _in_proj_body(x_ref, g_ref, w_ref, o_ref, xn_ref):
    @pl.when(pl.program_id(1) == 0)
    def _():
        xn_ref[...] = _rms(x_ref[...], g_ref[...]).astype(BF16)

    o_ref[...] = _bdot(xn_ref[...], w_ref[...])


def in_proj(x, g, w, *, tm_pref=512, tn_pref=1536):
    m, d = x.shape
    n = w.shape[1]
    tm, tn = _tile(m, tm_pref), _tile(n, tn_pref)
    return pl.pallas_call(
        _in_proj_body,
        grid=(m // tm, n // tn),
        in_specs=[
            pl.BlockSpec((tm, d), lambda i, j: (i, 0)),
            pl.BlockSpec((1, d), lambda i, j: (0, 0)),
            pl.BlockSpec((d, tn), lambda i, j: (0, j)),
        ],
        out_specs=pl.BlockSpec((tm, tn), lambda i, j: (i, j)),
        out_shape=jax.ShapeDtypeStruct((m, n), F32),
        scratch_shapes=[pltpu.VMEM((tm, d), BF16)],
        compiler_params=_cparams("parallel", "arbitrary"),
        name="in_proj",
    )(x, g, w)


def _merge_body(x_ref, g_ref, wgm_ref, wga_ref, bm_ref, ba_ref, ym_ref, ya_ref, wo_ref,
                o_ref, xn_ref, mix_ref, *, tn):
    j = pl.program_id(1)

    @pl.when(j == 0)
    def _():
        xn_ref[...] = _rms(x_ref[...], g_ref[...]).astype(BF16)

    xn = xn_ref[...]
    gm = jax.nn.sigmoid(_bdot(xn, wgm_ref[...]) + bm_ref[...])
    ga = jax.nn.sigmoid(_bdot(xn, wga_ref[...]) + ba_ref[...])
    col = pl.multiple_of(j * tn, LANE)
    mix_ref[:, pl.ds(col, tn)] = (gm * ym_ref[...] + ga * ya_ref[...]).astype(BF16)

    @pl.when(j == pl.num_programs(1) - 1)
    def _():
        o_ref[...] = x_ref[...] + _bdot(mix_ref[...], wo_ref[...])


def merge(x, g, wgm, wga, bm, ba, ym, ya, wo, *, tm_pref=256, tn_pref=512):
    m, d = x.shape
    tm, tn = _tile(m, tm_pref), _tile(d, tn_pref)
    return pl.pallas_call(
        functools.partial(_merge_body, tn=tn),
        grid=(m // tm, d // tn),
        in_specs=[
            pl.BlockSpec((tm, d), lambda i, j: (i, 0)),
            pl.BlockSpec((1, d), lambda i, j: (0, 0)),
            pl.BlockSpec((d, tn), lambda i, j: (0, j)),
            pl.BlockSpec((d, tn), lambda i, j: (0, j)),
            pl.BlockSpec((1, tn), lambda i, j: (0, j)),
            pl.BlockSpec((1, tn), lambda i, j: (0, j)),
            pl.BlockSpec((tm, tn), lambda i, j: (i, j)),
            pl.BlockSpec((tm, tn), lambda i, j: (i, j)),
            pl.BlockSpec((d, d), lambda i, j: (0, 0), pipeline_mode=pl.Buffered(1)),
        ],
        out_specs=pl.BlockSpec((tm, d), lambda i, j: (i, 0)),
        out_shape=jax.ShapeDtypeStruct((m, d), F32),
        scratch_shapes=[pltpu.VMEM((tm, d), BF16), pltpu.VMEM((tm, d), BF16)],
        compiler_params=_cparams("parallel", "arbitrary"),
        name="merge",
    )(x, g, wgm, wga, bm, ba, ym, ya, wo)


def _ple_body(x_ref, xt_ref, g_ref, wpg_ref, p_ref, wple_ref, o_ref, xn_ref):
    @pl.when(pl.program_id(1) == 0)
    def _():
        xn_ref[...] = _rms(x_ref[...], g_ref[...]).astype(BF16)

    gate = jax.nn.sigmoid(_bdot(xn_ref[...], wpg_ref[...]))
    emb = _bdot(p_ref[...].astype(BF16), wple_ref[...])
    o_ref[...] = xt_ref[...] + gate * emb


def ple(x, g, wpg, p, wple, *, tm_pref=512, tn_pref=1024):
    m, d = x.shape
    pd = p.shape[1]
    tm, tn = _tile(m, tm_pref), _tile(d, tn_pref)
    return pl.pallas_call(
        _ple_body,
        grid=(m // tm, d // tn),
        in_specs=[
            pl.BlockSpec((tm, d), lambda i, j: (i, 0)),
            pl.BlockSpec((tm, tn), lambda i, j: (i, j)),
            pl.BlockSpec((1, d), lambda i, j: (0, 0)),
            pl.BlockSpec((d, tn), lambda i, j: (0, j)),
            pl.BlockSpec((tm, pd), lambda i, j: (i, 0)),
            pl.BlockSpec((pd, tn), lambda i, j: (0, j)),
        ],
        out_specs=pl.BlockSpec((tm, tn), lambda i, j: (i, j)),
        out_shape=jax.ShapeDtypeStruct((m, d), F32),
        scratch_shapes=[pltpu.VMEM((tm, d), BF16)],
        compiler_params=_cparams("parallel", "arbitrary"),
        name="ple",
    )(x, x, g, wpg, p, wple)


def _mlstm_chunk_body(q_ref, k_ref, v_ref, zo_ref, gc_ref, gr_ref, bc_ref, br_ref, gmh_ref,
                      ym_ref, c_out, n_out, m_out, c_scr, n_scr, m_scr, *, nh, dqk, dv, cl):
    c = pl.program_id(1)

    @pl.when(c == 0)
    def _():
        c_scr[...] = jnp.zeros_like(c_scr)
        n_scr[...] = jnp.zeros_like(n_scr)
        m_scr[...] = jnp.zeros_like(m_scr)

    gc = gc_ref[0] + bc_ref[...]
    gr = gr_ref[0, 0] + br_ref[...]
    t_idx = lax.broadcasted_iota(jnp.int32, (cl, cl), 0)
    j_idx = lax.broadcasted_iota(jnp.int32, (cl, cl), 1)
    causal = j_idx <= t_idx
    for h in range(nh):
        i_col = gc[:, h:h + 1]
        f_col = _log_sigmoid(gc[:, nh + h:nh + h + 1])
        i_row = gr[h:h + 1, :]
        f_row = _log_sigmoid(gr[nh + h:nh + h + 1, :])
        b_col = jnp.sum(jnp.where(causal, f_row, 0.0), axis=1, keepdims=True)
        b_row = jnp.sum(jnp.where(t_idx <= j_idx, f_col, 0.0), axis=0, keepdims=True)
        m_prev = m_scr[h:h + 1, 0:1]
        d = jnp.where(causal, b_col - b_row + i_row, -jnp.inf)
        inter = b_col + m_prev
        m_t = jnp.maximum(inter, jnp.max(d, axis=1, keepdims=True))
        w_inter = jnp.exp(inter - m_t)
        qh = q_ref[0, :, h * dqk:(h + 1) * dqk]
        kh = k_ref[0, :, h * dqk:(h + 1) * dqk]
        vb = v_ref[0, :, h * dv:(h + 1) * dv].astype(BF16)
        qb = qh.astype(BF16)
        s = lax.dot_general(qb, kh.astype(BF16), _NT, preferred_element_type=F32) * jnp.exp(d - m_t)
        c_old = c_scr[h]
        n_old = n_scr[h:h + 1, :]
        num = w_inter * _bdot(qb, c_old.astype(BF16)) + _bdot(s.astype(BF16), vb)
        den = w_inter * jnp.sum(qh * n_old, axis=1, keepdims=True) + jnp.sum(s, axis=1, keepdims=True)
        hh = num / jnp.maximum(jnp.abs(den), jnp.exp(-m_t))
        hn = _rms(hh, gmh_ref[:, h * dv:(h + 1) * dv])
        ym_ref[0, :, h * dv:(h + 1) * dv] = jax.nn.sigmoid(zo_ref[0, :, h * dv:(h + 1) * dv]) * hn
        b_last = b_col[cl - 1:cl, :]
        g_col = b_last - b_col + i_col
        g_row = b_last - b_row + i_row
        m_new = jnp.maximum(b_last + m_prev, jnp.max(g_row, axis=1, keepdims=True))
        a_prev = jnp.exp(b_last + m_prev - m_new)
        ka = kh * jnp.exp(g_col - m_new)
        c_scr[h] = a_prev * c_old + _bdot(ka.T.astype(BF16), vb)
        n_scr[h:h + 1, :] = a_prev * n_old + jnp.sum(ka, axis=0, keepdims=True)
        m_scr[h:h + 1, :] = jnp.broadcast_to(m_new, (1, LANE))

    @pl.when(c == pl.num_programs(1) - 1)
    def _():
        c_out[0] = c_scr[...]
        n_out[0] = n_scr[...]
        m_out[0] = m_scr[...]


def mlstm_prompt(z3, gc, gr, bc, br, gmh, *, nh, dqk, dv, cl):
    bsz, s, _ = z3.shape
    nc = s // cl
    wq, wv = nh * dqk, nh * dv
    assert wq % LANE == 0 and wv % wq == 0
    kq, kv = wq // wq, wv // wq
    v_blk = (2 * wq) // wv
    assert (2 * wq) % wv == 0
    body = functools.partial(_mlstm_chunk_body, nh=nh, dqk=dqk, dv=dv, cl=cl)
    return pl.pallas_call(
        body,
        grid=(bsz, nc),
        in_specs=[
            pl.BlockSpec((1, cl, wq), lambda b, c: (b, c, 0)),
            pl.BlockSpec((1, cl, wq), lambda b, c: (b, c, kq)),
            pl.BlockSpec((1, cl, wv), lambda b, c: (b, c, v_blk)),
            pl.BlockSpec((1, cl, wv), lambda b, c: (b, c, v_blk + 1)),
            pl.BlockSpec((1, cl, 2 * nh), lambda b, c: (b, c, 0)),
            pl.BlockSpec((1, 1, 2 * nh, cl), lambda b, c: (b, c, 0, 0)),
            pl.BlockSpec((1, 2 * nh), lambda b, c: (0, 0)),
            pl.BlockSpec((2 * nh, 1), lambda b, c: (0, 0)),
            pl.BlockSpec((1, wv), lambda b, c: (0, 0)),
        ],
        out_specs=[
            pl.BlockSpec((1, cl, wv), lambda b, c: (b, c, 0)),
            pl.BlockSpec((1, nh, dqk, dv), lambda b, c: (b, 0, 0, 0)),
            pl.BlockSpec((1, nh, dqk), lambda b, c: (b, 0, 0)),
            pl.BlockSpec((1, nh, LANE), lambda b, c: (b, 0, 0)),
        ],
        out_shape=[
            jax.ShapeDtypeStruct((bsz, s, wv), F32),
            jax.ShapeDtypeStruct((bsz, nh, dqk, dv), F32),
            jax.ShapeDtypeStruct((bsz, nh, dqk), F32),
            jax.ShapeDtypeStruct((bsz, nh, LANE), F32),
        ],
        scratch_shapes=[pltpu.VMEM((nh, dqk, dv), F32), pltpu.VMEM((nh, dqk), F32),
                        pltpu.VMEM((nh, LANE), F32)],
        compiler_params=_cparams("parallel", "arbitrary"),
        name="mlstm_prompt",
    )(z3, z3, z3, z3, gc, gr, bc, br, gmh)


def _mlstm_step_body(q_ref, k_ref, v_ref, zo_ref, gs_ref, bias_ref, gmh_ref, c_ref, n_ref, m_ref,
                     ym_ref, c_out, n_out, m_out, *, nh, dqk, dv):
    gates = gs_ref[0] + bias_ref[...]
    eye = (lax.broadcasted_iota(jnp.int32, (dqk, dqk), 0)
           == lax.broadcasted_iota(jnp.int32, (dqk, dqk), 1))

    def as_column(row):
        return jnp.sum(jnp.where(eye, row, 0.0), axis=1, keepdims=True)

    for h in range(nh):
        i_pre = gates[:, h:h + 1]
        logf = _log_sigmoid(gates[:, nh + h:nh + h + 1])
        m_prev = m_ref[0, :, h:h + 1]
        inter = logf + m_prev
        m_t = jnp.maximum(inter, i_pre)
        a_prev = jnp.exp(inter - m_t)
        a_new = jnp.exp(i_pre - m_t)
        qh = q_ref[0, :, h * dqk:(h + 1) * dqk]
        kh = k_ref[0, :, h * dqk:(h + 1) * dqk] * a_new
        vh = v_ref[0, :, h * dv:(h + 1) * dv]
        c_new = a_prev * c_ref[0, h] + as_column(kh) * vh
        n_new = a_prev * n_ref[0, h:h + 1, :] + kh
        num = jnp.sum(as_column(qh) * c_new, axis=0, keepdims=True)
        den = jnp.sum(qh * n_new, axis=1, keepdims=True)
        hh = num / jnp.maximum(jnp.abs(den), jnp.exp(-m_t))
        hn = _rms(hh, gmh_ref[:, h * dv:(h + 1) * dv])
        ym_ref[0, :, h * dv:(h + 1) * dv] = jax.nn.sigmoid(zo_ref[0, :, h * dv:(h + 1) * dv]) * hn
        c_out[0, h] = c_new
        n_out[0, h:h + 1, :] = n_new
        m_out[0, :, h:h + 1] = m_t


def mlstm_step(z3, gs, bias, gmh, c0, n0, m0, *, nh, dqk, dv):
    db = z3.shape[0]
    wq, wv = nh * dqk, nh * dv
    v_blk = (2 * wq) // wv
    body = functools.partial(_mlstm_step_body, nh=nh, dqk=dqk, dv=dv)
    m3 = m0.reshape(db, 1, nh)
    return pl.pallas_call(
        body,
        grid=(db,),
        in_specs=[
            pl.BlockSpec((1, 1, wq), lambda s: (s, 0, 0)),
            pl.BlockSpec((1, 1, wq), lambda s: (s, 0, 1)),
            pl.BlockSpec((1, 1, wv), lambda s: (s, 0, v_blk)),
            pl.BlockSpec((1, 1, wv), lambda s: (s, 0, v_blk + 1)),
            pl.BlockSpec((1, 1, 2 * nh), lambda s: (s, 0, 0)),
            pl.BlockSpec((1, 2 * nh), lambda s: (0, 0)),
            pl.BlockSpec((1, wv), lambda s: (0, 0)),
            pl.BlockSpec((1, nh, dqk, dv), lambda s: (s, 0, 0, 0)),
            pl.BlockSpec((1, nh, dqk), lambda s: (s, 0, 0)),
            pl.BlockSpec((1, 1, nh), lambda s: (s, 0, 0)),
        ],
        out_specs=[
            pl.BlockSpec((1, 1, wv), lambda s: (s, 0, 0)),
            pl.BlockSpec((1, nh, dqk, dv), lambda s: (s, 0, 0, 0)),
            pl.BlockSpec((1, nh, dqk), lambda s: (s, 0, 0)),
            pl.BlockSpec((1, 1, nh), lambda s: (s, 0, 0)),
        ],
        out_shape=[
            jax.ShapeDtypeStruct((db, 1, wv), F32),
            jax.ShapeDtypeStruct((db, nh, dqk, dv), F32),
            jax.ShapeDtypeStruct((db, nh, dqk), F32),
            jax.ShapeDtypeStruct((db, 1, nh), F32),
        ],
        compiler_params=_cparams("parallel"),
        name="mlstm_step",
    )(z3, z3, z3, z3, gs, bias, gmh, c0, n0, m3)


def _rope_half(xb, cs):
    t = xb * cs
    return t + pltpu.roll(t, 64, axis=1)


def _query_heads(cq_b, wq_ref, cs, gq_ref, h, nope, qk_head, low):
    w = 2 * LANE
    qf = _bdot(cq_b, wq_ref[:, h * w:(h + 1) * w])
    qn = qf[:, :nope]
    qp = jnp.where(low, _rope_half(qf[:, nope:], cs), 0.0)
    ssq = jnp.sum(qn * qn, axis=1, keepdims=True) + jnp.sum(qp * qp, axis=1, keepdims=True)
    r = lax.rsqrt(ssq / qk_head + EPS) * (qk_head ** -0.5)
    return qn * r * gq_ref[:, :nope], qp * r * gq_ref[:, nope:]


def _key_heads(ckv_b, wkv_ref, kpe, kpe_sq, gk_ref, h, nope, qk_head):
    w = 2 * LANE
    kvf = _bdot(ckv_b, wkv_ref[:, h * w:(h + 1) * w])
    kn = kvf[:, :nope]
    r = lax.rsqrt((jnp.sum(kn * kn, axis=1, keepdims=True) + kpe_sq) / qk_head + EPS)
    return kn * r * gk_ref[:, :nope], kpe * r * gk_ref[:, nope:], kvf[:, nope:]


def _mla_prep_body(zcq_ref, zckv_ref, zkr_ref, cs_ref, gcq_ref, gckv_ref, gq_ref, gk_ref,
                   wq_ref, wkv_ref, q_out, k_out, v_out, ckv_out, kpe_out, *, nheads, nope, rope):
    qk_head = nope + rope
    cs = cs_ref[...]
    tm = cs.shape[0]
    low = lax.broadcasted_iota(jnp.int32, (tm, LANE), 1) < rope
    cq_b = _rms(zcq_ref[0], gcq_ref[...]).astype(BF16)
    ckv = _rms(zckv_ref[0], gckv_ref[...])
    ckv_out[0] = ckv
    ckv_b = ckv.astype(BF16)
    kpe = jnp.where(low, _rope_half(zkr_ref[0], cs), 0.0)
    kpe_out[0] = kpe[:, :rope]
    kpe_sq = jnp.sum(kpe * kpe, axis=1, keepdims=True)
    for h in range(nheads):
        qn, qp = _query_heads(cq_b, wq_ref, cs, gq_ref, h, nope, qk_head, low)
        q_out[0, h, :, :nope] = qn.astype(BF16)
        q_out[0, h, :, nope:] = qp.astype(BF16)
        kn, kp, v = _key_heads(ckv_b, wkv_ref, kpe, kpe_sq, gk_ref, h, nope, qk_head)
        k_out[0, h, :, :nope] = kn.astype(BF16)
        k_out[0, h, :, nope:] = kp.astype(BF16)
        v_out[0, h] = v.astype(BF16)


def mla_prep(z3, cs, gcq, gckv, gq, gk, wq, wkv, *, col_cq, nheads, nope, rope, av, tm_pref=256):
    bsz, s, _ = z3.shape
    ql, kvl = wq.shape[0], wkv.shape[0]
    assert nope == LANE and 2 * rope == LANE and av == LANE and ql == kvl
    tm = _tile(s, tm_pref)
    cb = col_cq // ql
    kr_blk = (col_cq + ql + kvl) // LANE
    body = functools.partial(_mla_prep_body, nheads=nheads, nope=nope, rope=rope)
    hw = 2 * LANE
    return pl.pallas_call(
        body,
        grid=(bsz, s // tm),
        in_specs=[
            pl.BlockSpec((1, tm, ql), lambda b, i: (b, i, cb)),
            pl.BlockSpec((1, tm, kvl), lambda b, i: (b, i, cb + 1)),
            pl.BlockSpec((1, tm, LANE), lambda b, i: (b, i, kr_blk)),
            pl.BlockSpec((tm, LANE), lambda b, i: (i, 0)),
            pl.BlockSpec((1, ql), lambda b, i: (0, 0)),
            pl.BlockSpec((1, kvl), lambda b, i: (0, 0)),
            pl.BlockSpec((1, hw), lambda b, i: (0, 0)),
            pl.BlockSpec((1, hw), lambda b, i: (0, 0)),
            pl.BlockSpec((ql, nheads * hw), lambda b, i: (0, 0)),
            pl.BlockSpec((kvl, nheads * hw), lambda b, i: (0, 0)),
        ],
        out_specs=[
            pl.BlockSpec((1, nheads, tm, hw), lambda b, i: (b, 0, i, 0)),
            pl.BlockSpec((1, nheads, tm, hw), lambda b, i: (b, 0, i, 0)),
            pl.BlockSpec((1, nheads, tm, av), lambda b, i: (b, 0, i, 0)),
            pl.BlockSpec((1, tm, kvl), lambda b, i: (b, i, 0)),
            pl.BlockSpec((1, tm, rope), lambda b, i: (b, i, 0)),
        ],
        out_shape=[
            jax.ShapeDtypeStruct((bsz, nheads, s, hw), BF16),
            jax.ShapeDtypeStruct((bsz, nheads, s, hw), BF16),
            jax.ShapeDtypeStruct((bsz, nheads, s, av), BF16),
            jax.ShapeDtypeStruct((bsz, s, kvl), F32),
            jax.ShapeDtypeStruct((bsz, s, rope), F32),
        ],
        compiler_params=_cparams("parallel", "parallel"),
        name="mla_prep",
    )(z3, z3, z3, cs, gcq, gckv, gq, gk, wq, wkv)


def _flash_body(q_ref, k_ref, v_ref, o_ref, m_scr, l_scr, acc_scr, *, tq, tk):
    qi = pl.program_id(2)
    ki = pl.program_id(3)

    @pl.when(ki == 0)
    def _():
        m_scr[...] = jnp.full_like(m_scr, -jnp.inf)
        l_scr[...] = jnp.zeros_like(l_scr)
        acc_scr[...] = jnp.zeros_like(acc_scr)

    def update(masked):
        s = lax.dot_general(q_ref[0, 0], k_ref[0, 0], _NT, preferred_element_type=F32)
        if masked:
            qpos = qi * tq + lax.broadcasted_iota(jnp.int32, (tq, tk), 0)
            kpos = ki * tk + lax.broadcasted_iota(jnp.int32, (tq, tk), 1)
            s = jnp.where(kpos <= qpos, s, -jnp.inf)
        m_old = m_scr[...]
        m_new = jnp.maximum(m_old, jnp.max(s, axis=1, keepdims=True))
        alpha = jnp.exp(m_old - m_new)
        p = jnp.exp(s - m_new)
        l_scr[...] = alpha * l_scr[...] + jnp.sum(p, axis=1, keepdims=True)
        acc_scr[...] = alpha * acc_scr[...] + _bdot(p.astype(BF16), v_ref[0, 0])
        m_scr[...] = m_new

    @pl.when(ki < qi)
    def _():
        update(False)

    @pl.when(ki == qi)
    def _():
        update(True)
        o_ref[0] = acc_scr[...] / l_scr[...]


def flash_attention(q, k, v, *, t_pref=512):
    bsz, nheads, s, hw = q.shape
    av = v.shape[-1]
    t = _tile(s, t_pref)
    nb = s // t
    body = functools.partial(_flash_body, tq=t, tk=t)
    return pl.pallas_call(
        body,
        grid=(bsz, nheads, nb, nb),
        in_specs=[
            pl.BlockSpec((1, 1, t, hw), lambda b, h, i, j: (b, h, i, 0)),
            pl.BlockSpec((1, 1, t, hw), lambda b, h, i, j: (b, h, jnp.minimum(i, j), 0)),
            pl.BlockSpec((1, 1, t, av), lambda b, h, i, j: (b, h, jnp.minimum(i, j), 0)),
        ],
        out_specs=pl.BlockSpec((1, t, av), lambda b, h, i, j: (b, i, h)),
        out_shape=jax.ShapeDtypeStruct((bsz, s, nheads * av), F32),
        scratch_shapes=[pltpu.VMEM((t, 1), F32), pltpu.VMEM((t, 1), F32), pltpu.VMEM((t, av), F32)],
        compiler_params=_cparams("parallel", "parallel", "parallel", "arbitrary"),
        name="flash_attention",
    )(q, k, v)


def _sample_prep_body(zcq_ref, zckv_ref, zkr_ref, cs_ref, gcq_ref, gckv_ref, gq_ref, gk_ref,
                      wq_ref, wkv_ref, wukt_ref, qa_out, qp_out, self_out, v_out, ckv_out, kpe_out,
                      *, nheads, nope, rope):
    qk_head = nope + rope
    cs = cs_ref[...]
    tm = cs.shape[0]
    low = lax.broadcasted_iota(jnp.int32, (tm, LANE), 1) < rope
    cq_b = _rms(zcq_ref[...], gcq_ref[...]).astype(BF16)
    ckv = _rms(zckv_ref[...], gckv_ref[...])
    ckv_out[...] = ckv
    ckv_b = ckv.astype(BF16)
    kpe = jnp.where(low, _rope_half(zkr_ref[...], cs), 0.0)
    kpe_out[...] = kpe[:, :rope]
    kpe_sq = jnp.sum(kpe * kpe, axis=1, keepdims=True)
    for h in range(nheads):
        qn, qp = _query_heads(cq_b, wq_ref, cs, gq_ref, h, nope, qk_head, low)
        kn, kp, v = _key_heads(ckv_b, wkv_ref, kpe, kpe_sq, gk_ref, h, nope, qk_head)
        self_out[h] = jnp.broadcast_to(
            jnp.sum(qn * kn, axis=1, keepdims=True) + jnp.sum(qp * kp, axis=1, keepdims=True), (tm, LANE))
        v_out[h] = v
        qg = (qn * gk_ref[:, :nope]).astype(BF16)
        qa_out[h] = _bdot(qg, wukt_ref[h * nope:(h + 1) * nope, :]).astype(BF16)
        qp_out[h] = (qp * gk_ref[:, nope:])[:, :rope].astype(BF16)


def sample_prep(z, cs, gcq, gckv, gq, gk, wq, wkv, wukt, *, col_cq, nheads, nope, rope, av):
    db = z.shape[0]
    ql, kvl = wq.shape[0], wkv.shape[0]
    cb = col_cq // ql
    kr_blk = (col_cq + ql + kvl) // LANE
    hw = 2 * LANE
    body = functools.partial(_sample_prep_body, nheads=nheads, nope=nope, rope=rope)
    full = lambda shape: pl.BlockSpec(shape, lambda i: (0,) * len(shape))
    return pl.pallas_call(
        body,
        grid=(1,),
        in_specs=[
            pl.BlockSpec((db, ql), lambda i: (0, cb)),
            pl.BlockSpec((db, kvl), lambda i: (0, cb + 1)),
            pl.BlockSpec((db, LANE), lambda i: (0, kr_blk)),
            full((db, LANE)), full((1, ql)), full((1, kvl)), full((1, hw)), full((1, hw)),
            full((ql, nheads * hw)), full((kvl, nheads * hw)), full((nheads * nope, kvl)),
        ],
        out_specs=[
            full((nheads, db, kvl)), full((nheads, db, rope)), full((nheads, db, LANE)),
            full((nheads, db, av)), full((db, kvl)), full((db, rope)),
        ],
        out_shape=[
            jax.ShapeDtypeStruct((nheads, db, kvl), BF16),
            jax.ShapeDtypeStruct((nheads, db, rope), BF16),
            jax.ShapeDtypeStruct((nheads, db, LANE), F32),
            jax.ShapeDtypeStruct((nheads, db, av), F32),
            jax.ShapeDtypeStruct((db, kvl), F32),
            jax.ShapeDtypeStruct((db, rope), F32),
        ],
        compiler_params=_cparams("arbitrary"),
        name="sample_prep",
    )(z, z, z, cs, gcq, gckv, gq, gk, wq, wkv, wukt)


def _decode_body(pt_ref, qa_ref, qp_ref, wukt_ref, *refs, npg, nheads, nope, rope, page, sub):
    ckv_refs = refs[:npg]
    kpe_refs = refs[npg:2 * npg]
    acc_out, m_out, l_out = refs[2 * npg:2 * npg + 3]
    lhs_scr, cb_scr, kp_scr, m_scr, l_scr, acc_scr = refs[2 * npg + 3:]
    del pt_ref
    j = pl.program_id(1)
    qk_head = nope + rope
    nk = nheads * nope

    @pl.when(j == 0)
    def _():
        lhs_scr[0:nk, :] = wukt_ref[...]
        lhs_scr[nk:nk + nheads, :] = qa_ref[0]
        m_scr[...] = jnp.full_like(m_scr, -jnp.inf)
        l_scr[...] = jnp.zeros_like(l_scr)
        acc_scr[...] = jnp.zeros_like(acc_scr)

    for p in range(npg):
        cb_scr[p * page:(p + 1) * page, :] = ckv_refs[p][0, 0].astype(BF16)
        kp_scr[p * page:(p + 1) * page, :] = kpe_refs[p][0, 0]

    ones = jnp.ones((8, rope), BF16)
    qp = qp_ref[0]

    def step(t, carry):
        off = pl.multiple_of(t * sub, sub)
        cb = cb_scr[pl.ds(off, sub), :]
        kp = kp_scr[pl.ds(off, sub), :]
        r = lax.dot_general(lhs_scr[...], cb, _NT, preferred_element_type=F32)
        kn = r[0:nk, :].reshape(nheads, nope, sub)
        ssq = jnp.sum(kn * kn, axis=1)
        kp2 = kp * kp
        hi = kp2.astype(BF16)
        lo = (kp2 - hi.astype(F32)).astype(BF16)
        pe_sq = (lax.dot_general(ones, hi, _NT, preferred_element_type=F32)
                 + lax.dot_general(ones, lo, _NT, preferred_element_type=F32))[0:1, :]
        s_pe = lax.dot_general(qp, kp.astype(BF16), _NT, preferred_element_type=F32)
        logits = (r[nk:nk + nheads, :] + s_pe) * lax.rsqrt((ssq + pe_sq) / qk_head + EPS)
        m_old = m_scr[...]
        m_new = jnp.maximum(m_old, jnp.max(logits, axis=1, keepdims=True))
        alpha = jnp.exp(m_old - m_new)
        pr = jnp.exp(logits - m_new)
        l_scr[...] = alpha * l_scr[...] + jnp.sum(pr, axis=1, keepdims=True)
        acc_scr[...] = alpha * acc_scr[...] + _bdot(pr.astype(BF16), cb)
        m_scr[...] = m_new
        return carry

    lax.fori_loop(0, (npg * page) // sub, step, 0)

    @pl.when(j == pl.num_programs(1) - 1)
    def _():
        acc_out[0] = acc_scr[...]
        m_out[0] = jnp.broadcast_to(m_scr[...], (nheads, LANE))
        l_out[0] = jnp.broadcast_to(l_scr[...], (nheads, LANE))


def paged_decode(pt_flat, qa, qp, wukt, cache_ckv, cache_kpe, layer, *, n_pages, nheads, nope, rope,
                 npg=16, sub=256):
    db = qa.shape[0]
    page, kvl = cache_ckv.shape[2], cache_ckv.shape[3]
    npg = math.gcd(npg, n_pages)
    if (npg * page) % sub:
        sub = page
    nk = nheads * nope
    body = functools.partial(_decode_body, npg=npg, nheads=nheads, nope=nope, rope=rope, page=page, sub=sub)

    def page_spec(width, p):
        return pl.BlockSpec((1, 1, page, width),
                            lambda s, j, pt: (layer, pt[s * n_pages + j * npg + p], 0, 0))

    in_specs = [
        pl.BlockSpec((1, nheads, kvl), lambda s, j, pt: (s, 0, 0)),
        pl.BlockSpec((1, nheads, rope), lambda s, j, pt: (s, 0, 0)),
        pl.BlockSpec((nk, kvl), lambda s, j, pt: (0, 0)),
    ]
    in_specs += [page_spec(kvl, p) for p in range(npg)]
    in_specs += [page_spec(rope, p) for p in range(npg)]
    out_spec = lambda w: pl.BlockSpec((1, nheads, w), lambda s, j, pt: (s, 0, 0))
    grid_spec = pltpu.PrefetchScalarGridSpec(
        num_scalar_prefetch=1,
        grid=(db, n_pages // npg),
        in_specs=in_specs,
        out_specs=[out_spec(kvl), out_spec(LANE), out_spec(LANE)],
        scratch_shapes=[
            pltpu.VMEM((nk + nheads, kvl), BF16),
            pltpu.VMEM((npg * page, kvl), BF16),
            pltpu.VMEM((npg * page, rope), F32),
            pltpu.VMEM((nheads, 1), F32),
            pltpu.VMEM((nheads, 1), F32),
            pltpu.VMEM((nheads, kvl), F32),
        ],
    )
    return pl.pallas_call(
        body,
        grid_spec=grid_spec,
        out_shape=[
            jax.ShapeDtypeStruct((db, nheads, kvl), F32),
            jax.ShapeDtypeStruct((db, nheads, LANE), F32),
            jax.ShapeDtypeStruct((db, nheads, LANE), F32),
        ],
        compiler_params=_cparams("arbitrary", "arbitrary"),
        name="paged_decode",
    )(pt_flat, qa, qp, wukt, *([cache_ckv] * npg), *([cache_kpe] * npg))


def _decode_finish_body(acc_ref, m_ref, l_ref, self_ref, v_ref, wuv_ref, o_ref, *, nheads, av):
    for h in range(nheads):
        m_old = m_ref[h][:, 0:1]
        ls = self_ref[h][:, 0:1]
        m_new = jnp.maximum(m_old, ls)
        alpha = jnp.exp(m_old - m_new)
        ps = jnp.exp(ls - m_new)
        denom = alpha * l_ref[h][:, 0:1] + ps
        ctx = (acc_ref[h] * alpha).astype(BF16)
        o_ref[:, h * av:(h + 1) * av] = (_bdot(ctx, wuv_ref[h]) + ps * v_ref[h]) / denom


def decode_finish(acc, m, l, self_logit, v_self, wuv):
    nheads, db, kvl = acc.shape
    av = wuv.shape[-1]
    full = lambda shape: pl.BlockSpec(shape, lambda i: (0,) * len(shape))
    return pl.pallas_call(
        functools.partial(_decode_finish_body, nheads=nheads, av=av),
        grid=(1,),
        in_specs=[full(acc.shape), full(m.shape), full(l.shape), full(self_logit.shape),
                  full(v_self.shape), full(wuv.shape)],
        out_specs=full((db, nheads * av)),
        out_shape=jax.ShapeDtypeStruct((db, nheads * av), F32),
        compiler_params=_cparams("arbitrary"),
        name="decode_finish",
    )(acc, m, l, self_logit, v_self, wuv)


def _rot_cols(w):
    half = w.shape[-1] // 2
    return jnp.concatenate([-w[..., half:], w[..., :half]], axis=-1)


def _pad_cols(w, n):
    return jnp.pad(w, ((0, 0), (0, n - w.shape[1])))


def _layer_weights(l, dims, norm_g, w_ffn_up, w_ffn_down, w_in, b_if, g_mh, g_cq, w_uq, g_ckv, w_ukv,
                   g_q, g_k, w_gate, b_gate, w_out, w_ple, w_pg):
    d, dff, fp = dims["d"], dims["dff"], dims["fp"]
    nh, dqk, dv = dims["nh"], dims["dqk"], dims["dv"]
    ah, ql, kvl, nope, rope, av = dims["ah"], dims["ql"], dims["kvl"], dims["nope"], dims["rope"], dims["av"]
    w = {}
    w["g"] = norm_g[l].reshape(4, 1, d)
    for i in range(2):
        up, down = w_ffn_up[l, i], w_ffn_down[l, i]
        w[f"wa{i}"] = _pad_cols(up[:, :dff], fp).astype(BF16)
        w[f"wb{i}"] = _pad_cols(up[:, dff:], fp).astype(BF16)
        w[f"wd{i}"] = jnp.pad(down, ((0, fp - dff), (0, 0))).astype(BF16)
    wi = w_in[l]
    o = [0]
    for width in (nh * dqk, nh * dqk, nh * dv, d, nh, nh, ql, kvl, rope):
        o.append(o[-1] + width)
    zq, zk, zv, zo, zi, zf, zcq, zckv, zkr = (wi[:, o[i]:o[i + 1]] for i in range(9))
    cols = [zq, zk * (dqk ** -0.5), zv, zo, zcq, zckv, zkr, _rot_cols(zkr), zi, zf]
    win = jnp.concatenate(cols, axis=1)
    w["win"] = _pad_cols(win, dims["nin"]).astype(BF16)
    w["b_col"] = b_if[l].reshape(1, 2 * nh)
    w["b_row"] = b_if[l].reshape(2 * nh, 1)
    w["gmh"] = g_mh[l].reshape(1, nh * dv)
    w["gcq"] = g_cq[l].reshape(1, ql)
    w["gckv"] = g_ckv[l].reshape(1, kvl)
    zpad = jnp.zeros((rope,), F32)
    w["gq"] = jnp.concatenate([g_q[l], zpad]).reshape(1, 2 * LANE)
    w["gk"] = jnp.concatenate([g_k[l], zpad]).reshape(1, 2 * LANE)
    uq = w_uq[l]
    pe = uq[:, :, nope:]
    w["wq"] = jnp.concatenate([uq[:, :, :nope], pe, _rot_cols(pe)], axis=-1).reshape(ql, ah * 2 * LANE).astype(BF16)
    ukv = w_ukv[l]
    w["wkv"] = ukv.reshape(kvl, ah * (nope + av)).astype(BF16)
    w["wukt"] = ukv[:, :, :nope].reshape(kvl, ah * nope).T.astype(BF16)
    w["wuv"] = jnp.transpose(ukv[:, :, nope:], (1, 0, 2)).astype(BF16)
    w["wgm"] = w_gate[l][:, :d].astype(BF16)
    w["wga"] = w_gate[l][:, d:].astype(BF16)
    w["bgm"] = b_gate[l][:d].reshape(1, d)
    w["bga"] = b_gate[l][d:].reshape(1, d)
    w["wo"] = w_out[l].astype(BF16)
    w["wple"] = w_ple[l].astype(BF16)
    w["wpg"] = w_pg[l].astype(BF16)
    return w


def _rope_table(pos, rope):
    half = rope // 2
    inv = ROPE_THETA ** (-jnp.arange(half, dtype=F32) / half)
    ang = pos.astype(F32)[:, None] * inv[None, :]
    cos, sin = jnp.cos(ang), jnp.sin(ang)
    return jnp.concatenate([cos, cos, sin, sin], axis=1)


def kernel(x_prompt, x_sample, cache_ckv, cache_kpe, state_C, state_n, state_m, page_table,
           p_prompt, p_sample, norm_g, w_ffn_up, w_ffn_down, w_in, b_if, g_mh, g_cq, w_uq,
           g_ckv, w_ukv, g_q, g_k, w_gate, b_gate, w_out, w_ple, w_pg):
    bsz, seq, d = x_prompt.shape
    db, dseq, _ = x_sample.shape
    assert dseq == 1
    depth, _, nh, dqk, dv = state_C.shape
    _, ql, ah, qk_head = w_uq.shape
    kvl = w_ukv.shape[1]
    rope = cache_kpe.shape[-1]
    nope = qk_head - rope
    av = w_ukv.shape[-1] - nope
    dff = w_ffn_down.shape[2]
    page = cache_ckv.shape[2]
    n_pages = page_table.shape[1]
    past = n_pages * page
    pd = p_prompt.shape[-1]
    col_cq = 2 * nh * dqk + nh * dv + d
    n_used = col_cq + ql + kvl + 2 * rope + 2 * nh
    col_gate = col_cq + ql + kvl + 2 * rope
    tn_in = 1536 if d >= 1536 else 256
    dims = dict(d=d, dff=dff, fp=-(-dff // 512) * 512 if dff > 512 else dff, nh=nh, dqk=dqk, dv=dv, ah=ah,
                ql=ql, kvl=kvl, nope=nope, rope=rope, av=av, nin=-(-n_used // tn_in) * tn_in)
    cl = _tile(seq, 256)
    nc = seq // cl

    cs_p = _rope_table(jnp.arange(seq), rope)
    cs_s = jnp.broadcast_to(_rope_table(past + jnp.arange(1), rope), (db, 2 * rope))
    pt_flat = page_table.reshape(-1).astype(jnp.int32)

    xp = x_prompt.reshape(bsz * seq, d)
    xs = x_sample.reshape(db, d)
    pp = p_prompt.reshape(depth, bsz * seq, pd)
    ps = p_sample.reshape(depth, db, pd)
    outs_p, outs_s = [], []
    for l in range(depth):
        w = _layer_weights(l, dims, norm_g, w_ffn_up, w_ffn_down, w_in, b_if, g_mh, g_cq, w_uq, g_ckv,
                           w_ukv, g_q, g_k, w_gate, b_gate, w_out, w_ple, w_pg)
        mla = dict(col_cq=col_cq, nheads=ah, nope=nope, rope=rope, av=av)

        xp = ffn(xp, w["g"][0], w["wa0"], w["wb0"], w["wd0"])
        z = in_proj(xp, w["g"][1], w["win"], tn_pref=tn_in)
        z3 = z.reshape(bsz, seq, -1)
        gates = z3[:, :, col_gate:col_gate + 2 * nh]
        gates_r = jnp.swapaxes(gates.reshape(bsz, nc, cl, 2 * nh), 2, 3)
        ym, c_p, n_p, m_p = mlstm_prompt(z3, gates, gates_r, w["b_col"], w["b_row"], w["gmh"],
                                         nh=nh, dqk=dqk, dv=dv, cl=cl)
        qh, kh, vh, ckv_p, kpe_p = mla_prep(z3, cs_p, w["gcq"], w["gckv"], w["gq"], w["gk"],
                                            w["wq"], w["wkv"], **mla)
        ya = flash_attention(qh, kh, vh)
        xp = merge(xp, w["g"][1], w["wgm"], w["wga"], w["bgm"], w["bga"],
                   ym.reshape(bsz * seq, d), ya.reshape(bsz * seq, d), w["wo"])
        xp = ffn(xp, w["g"][2], w["wa1"], w["wb1"], w["wd1"])
        xp = ple(xp, w["g"][3], w["wpg"], pp[l], w["wple"])
        outs_p.append((ckv_p, kpe_p, c_p, n_p, m_p[:, :, 0]))

        xs = ffn(xs, w["g"][0], w["wa0"], w["wb0"], w["wd0"])
        zs = in_proj(xs, w["g"][1], w["win"], tn_pref=tn_in)
        zs3 = zs.reshape(db, 1, -1)
        ym_s, c_s, n_s, m_s = mlstm_step(zs3, zs3[:, :, col_gate:col_gate + 2 * nh], w["b_col"], w["gmh"],
                                         state_C[l], state_n[l], state_m[l], nh=nh, dqk=dqk, dv=dv)
        qa, qp, self_logit, v_self, ckv_s, kpe_s = sample_prep(
            zs, cs_s, w["gcq"], w["gckv"], w["gq"], w["gk"], w["wq"], w["wkv"], w["wukt"], **mla)
        acc, m_run, l_run = paged_decode(pt_flat, jnp.swapaxes(qa, 0, 1), jnp.swapaxes(qp, 0, 1), w["wukt"],
                                         cache_ckv, cache_kpe, l, n_pages=n_pages, nheads=ah,
                                         nope=nope, rope=rope)
        ya_s = decode_finish(jnp.swapaxes(acc, 0, 1), jnp.swapaxes(m_run, 0, 1), jnp.swapaxes(l_run, 0, 1),
                             self_logit, v_self, w["wuv"])
        xs = merge(xs, w["g"][1], w["wgm"], w["wga"], w["bgm"], w["bga"], ym_s.reshape(db, d), ya_s, w["wo"])
        xs = ffn(xs, w["g"][2], w["wa1"], w["wb1"], w["wd1"])
        xs = ple(xs, w["g"][3], w["wpg"], ps[l], w["wple"])
        outs_s.append((ckv_s.reshape(db, 1, kvl), kpe_s.reshape(db, 1, rope), c_s, n_s, m_s.reshape(db, nh)))

    stack = lambda outs, i: jnp.stack([o[i] for o in outs])
    return (xp.reshape(bsz, seq, d), xs.reshape(db, 1, d),
            stack(outs_p, 0), stack(outs_p, 1), stack(outs_p, 2), stack(outs_p, 3), stack(outs_p, 4),
            stack(outs_s, 0), stack(outs_s, 1), stack(outs_s, 2), stack(outs_s, 3), stack(outs_s, 4))
```

```python
import functools
import math

import jax
import jax.numpy as jnp
from jax import lax
from jax.experimental import pallas as pl
from jax.experimental.pallas import tpu as pltpu

F32 = jnp.float32
BF16 = jnp.bfloat16
EPS = 1e-6
ROPE_THETA = 10000.0
LANE = 128
VMEM_LIMIT_BYTES = 56 * 2**20
_NT = (((1,), (1,)), ((), ()))


def _cparams(*sem):
    return pltpu.CompilerParams(dimension_semantics=sem, vmem_limit_bytes=VMEM_LIMIT_BYTES)


def _rms(x, g):
    return x * lax.rsqrt(jnp.mean(x * x, axis=-1, keepdims=True) + EPS) * g


def _log_sigmoid(x):
    return jnp.minimum(x, 0.0) - jnp.log(1.0 + jnp.exp(-jnp.abs(x)))


def _bdot(a, b):
    return jnp.dot(a, b, preferred_element_type=F32)


def _tile(n, pref):
    if n <= pref:
        return n
    t = pref
    while n % t:
        t -= 8
    return t


def _ffn_body(x_ref, g_ref, wa_ref, wb_ref, wd_ref, o_ref, xn_ref, acc_ref):
    j = pl.program_id(1)

    @pl.when(j == 0)
    def _():
        xn_ref[...] = _rms(x_ref[...], g_ref[...]).astype(BF16)
        acc_ref[...] = jnp.zeros_like(acc_ref)

    xn = xn_ref[...]
    a = _bdot(xn, wa_ref[...])
    b = _bdot(xn, wb_ref[...])
    h = (a * jax.nn.sigmoid(a) * b).astype(BF16)
    acc_ref[...] += _bdot(h, wd_ref[...])

    @pl.when(j == pl.num_programs(1) - 1)
    def _():
        o_ref[...] = x_ref[...] + 0.5 * acc_ref[...]


def ffn(x, g, wa, wb, wd, *, tm_pref=512, tf_pref=512):
    m, d = x.shape
    fp = wa.shape[1]
    tm, tf = _tile(m, tm_pref), _tile(fp, tf_pref)
    return pl.pallas_call(
        _ffn_body,
        grid=(m // tm, fp // tf),
        in_specs=[
            pl.BlockSpec((tm, d), lambda i, j: (i, 0)),
            pl.BlockSpec((1, d), lambda i, j: (0, 0)),
            pl.BlockSpec((d, tf), lambda i, j: (0, j)),
            pl.BlockSpec((d, tf), lambda i, j: (0, j)),
            pl.BlockSpec((tf, d), lambda i, j: (j, 0)),
        ],
        out_specs=pl.BlockSpec((tm, d), lambda i, j: (i, 0)),
        out_shape=jax.ShapeDtypeStruct((m, d), F32),
        scratch_shapes=[pltpu.VMEM((tm, d), BF16), pltpu.VMEM((tm, d), F32)],
        compiler_params=_cparams("parallel", "arbitrary"),
        name="ffn",
    )(x, g, wa, wb, wd)


def _in_proj_body(x_ref, g_ref, w_ref, o_ref, xn_ref):
    @pl.when(pl.program_id(1) == 0)
    def _():
        xn_ref[...] = _rms(x_ref[...], g_ref[...]).astype(BF16)

    o_ref[...] = _bdot(xn_ref[...], w_ref[...])


def in_proj(x, g, w, *, tm_pref=512, tn_pref=1536):
    m, d = x.shape
    n = w.shape[1]
    tm, tn = _tile(m, tm_pref), _tile(n, tn_pref)
    return pl.pallas_call(
        _in_proj_body,
        grid=(m // tm, n // tn),
        in_specs=[
            pl.BlockSpec((tm, d), lambda i, j: (i, 0)),
            pl.BlockSpec((1, d), lambda i, j: (0, 0)),
            pl.BlockSpec((d, tn), lambda i, j: (0, j)),
        ],
        out_specs=pl.BlockSpec((tm, tn), lambda i, j: (i, j)),
        out_shape=jax.ShapeDtypeStruct((m, n), F32),
        scratch_shapes=[pltpu.VMEM((tm, d), BF16)],
        compiler_params=_cparams("parallel", "arbitrary"),
        name="in_proj",
    )(x, g, w)


def _merge_body(x_ref, g_ref, wgm_ref, wga_ref, bm_ref, ba_ref, ym_ref, ya_ref, wo_ref,
                o_ref, xn_ref, mix_ref, *, tn):
    j = pl.program_id(1)

    @pl.when(j == 0)
    def _():
        xn_ref[...] = _rms(x_ref[...], g_ref[...]).astype(BF16)

    xn = xn_ref[...]
    gm = jax.nn.sigmoid(_bdot(xn, wgm_ref[...]) + bm_ref[...])
    ga = jax.nn.sigmoid(_bdot(xn, wga_ref[...]) + ba_ref[...])
    col = pl.multiple_of(j * tn, LANE)
    mix_ref[:, pl.ds(col, tn)] = (gm * ym_ref[...] + ga * ya_ref[...]).astype(BF16)

    @pl.when(j == pl.num_programs(1) - 1)
    def _():
        o_ref[...] = x_ref[...] + _bdot(mix_ref[...], wo_ref[...])


def merge(x, g, wgm, wga, bm, ba, ym, ya, wo, *, tm_pref=512, tn_pref=512):
    m, d = x.shape
    tm, tn = _tile(m, tm_pref), _tile(d, tn_pref)
    return pl.pallas_call(
        functools.partial(_merge_body, tn=tn),
        grid=(m // tm, d // tn),
        in_specs=[
            pl.BlockSpec((tm, d), lambda i, j: (i, 0)),
            pl.BlockSpec((1, d), lambda i, j: (0, 0)),
            pl.BlockSpec((d, tn), lambda i, j: (0, j)),
            pl.BlockSpec((d, tn), lambda i, j: (0, j)),
            pl.BlockSpec((1, tn), lambda i, j: (0, j)),
            pl.BlockSpec((1, tn), lambda i, j: (0, j)),
            pl.BlockSpec((tm, tn), lambda i, j: (i, j)),
            pl.BlockSpec((tm, tn), lambda i, j: (i, j)),
            pl.BlockSpec((d, d), lambda i, j: (0, 0), pipeline_mode=pl.Buffered(1)),
        ],
        out_specs=pl.BlockSpec((tm, d), lambda i, j: (i, 0)),
        out_shape=jax.ShapeDtypeStruct((m, d), F32),
        scratch_shapes=[pltpu.VMEM((tm, d), BF16), pltpu.VMEM((tm, d), BF16)],
        compiler_params=_cparams("parallel", "arbitrary"),
        name="merge",
    )(x, g, wgm, wga, bm, ba, ym, ya, wo)


def _ple_body(x_ref, xt_ref, g_ref, wpg_ref, p_ref, wple_ref, o_ref, xn_ref):
    @pl.when(pl.program_id(1) == 0)
    def _():
        xn_ref[...] = _rms(x_ref[...], g_ref[...]).astype(BF16)

    gate = jax.nn.sigmoid(_bdot(xn_ref[...], wpg_ref[...]))
    emb = _bdot(p_ref[...].astype(BF16), wple_ref[...])
    o_ref[...] = xt_ref[...] + gate * emb


def ple(x, g, wpg, p, wple, *, tm_pref=512, tn_pref=1024):
    m, d = x.shape
    pd = p.shape[1]
    tm, tn = _tile(m, tm_pref), _tile(d, tn_pref)
    return pl.pallas_call(
        _ple_body,
        grid=(m // tm, d // tn),
        in_specs=[
            pl.BlockSpec((tm, d), lambda i, j: (i, 0)),
            pl.BlockSpec((tm, tn), lambda i, j: (i, j)),
            pl.BlockSpec((1, d), lambda i, j: (0, 0)),
            pl.BlockSpec((d, tn), lambda i, j: (0, j)),
            pl.BlockSpec((tm, pd), lambda i, j: (i, 0)),
            pl.BlockSpec((pd, tn), lambda i, j: (0, j)),
        ],
        out_specs=pl.BlockSpec((tm, tn), lambda i, j: (i, j)),
        out_shape=jax.ShapeDtypeStruct((m, d), F32),
        scratch_shapes=[pltpu.VMEM((tm, d), BF16)],
        compiler_params=_cparams("parallel", "arbitrary"),
        name="ple",
    )(x, x, g, wpg, p, wple)


def _mlstm_chunk_body(q_ref, k_ref, v_ref, zo_ref, gc_ref, gr_ref, bc_ref, br_ref, gmh_ref,
                      ym_ref, c_out, n_out, m_out, c_scr, n_scr, m_scr, *, nh, dqk, dv, cl):
    c = pl.program_id(1)

    @pl.when(c == 0)
    def _():
        c_scr[...] = jnp.zeros_like(c_scr)
        n_scr[...] = jnp.zeros_like(n_scr)
        m_scr[...] = jnp.zeros_like(m_scr)

    gc = gc_ref[0] + bc_ref[...]
    gr = gr_ref[0, 0] + br_ref[...]
    t_idx = lax.broadcasted_iota(jnp.int32, (cl, cl), 0)
    j_idx = lax.broadcasted_iota(jnp.int32, (cl, cl), 1)
    causal = j_idx <= t_idx
    for h in range(nh):
        i_col = gc[:, h:h + 1]
        f_col = _log_sigmoid(gc[:, nh + h:nh + h + 1])
        i_row = gr[h:h + 1, :]
        f_row = _log_sigmoid(gr[nh + h:nh + h + 1, :])
        b_col = jnp.sum(jnp.where(causal, f_row, 0.0), axis=1, keepdims=True)
        b_row = jnp.sum(jnp.where(t_idx <= j_idx, f_col, 0.0), axis=0, keepdims=True)
        m_prev = m_scr[h:h + 1, 0:1]
        d = jnp.where(causal, b_col - b_row + i_row, -jnp.inf)
        inter = b_col + m_prev
        m_t = jnp.maximum(inter, jnp.max(d, axis=1, keepdims=True))
        w_inter = jnp.exp(inter - m_t)
        qh = q_ref[0, :, h * dqk:(h + 1) * dqk]
        kh = k_ref[0, :, h * dqk:(h + 1) * dqk]
        vb = v_ref[0, :, h * dv:(h + 1) * dv].astype(BF16)
        qb = qh.astype(BF16)
        s = lax.dot_general(qb, kh.astype(BF16), _NT, preferred_element_type=F32) * jnp.exp(d - m_t)
        c_old = c_scr[h]
        n_old = n_scr[h:h + 1, :]
        num = w_inter * _bdot(qb, c_old.astype(BF16)) + _bdot(s.astype(BF16), vb)
        den = w_inter * jnp.sum(qh * n_old, axis=1, keepdims=True) + jnp.sum(s, axis=1, keepdims=True)
        hh = num / jnp.maximum(jnp.abs(den), jnp.exp(-m_t))
        hn = _rms(hh, gmh_ref[:, h * dv:(h + 1) * dv])
        ym_ref[0, :, h * dv:(h + 1) * dv] = jax.nn.sigmoid(zo_ref[0, :, h * dv:(h + 1) * dv]) * hn
        b_last = b_col[cl - 1:cl, :]
        g_col = b_last - b_col + i_col
        g_row = b_last - b_row + i_row
        m_new = jnp.maximum(b_last + m_prev, jnp.max(g_row, axis=1, keepdims=True))
        a_prev = jnp.exp(b_last + m_prev - m_new)
        ka = kh * jnp.exp(g_col - m_new)
        c_scr[h] = a_prev * c_old + _bdot(ka.T.astype(BF16), vb)
        n_scr[h:h + 1, :] = a_prev * n_old + jnp.sum(ka, axis=0, keepdims=True)
        m_scr[h:h + 1, :] = jnp.broadcast_to(m_new, (1, LANE))

    @pl.when(c == pl.num_programs(1) - 1)
    def _():
        c_out[0] = c_scr[...]
        n_out[0] = n_scr[...]
        m_out[0] = m_scr[...]


def mlstm_prompt(z3, gc, gr, bc, br, gmh, *, nh, dqk, dv, cl):
    bsz, s, _ = z3.shape
    nc = s // cl
    wq, wv = nh * dqk, nh * dv
    assert wq % LANE == 0 and wv % wq == 0
    kq, kv = wq // wq, wv // wq
    v_blk = (2 * wq) // wv
    assert (2 * wq) % wv == 0
    body = functools.partial(_mlstm_chunk_body, nh=nh, dqk=dqk, dv=dv, cl=cl)
    return pl.pallas_call(
        body,
        grid=(bsz, nc),
        in_specs=[
            pl.BlockSpec((1, cl, wq), lambda b, c: (b, c, 0)),
            pl.BlockSpec((1, cl, wq), lambda b, c: (b, c, kq)),
            pl.BlockSpec((1, cl, wv), lambda b, c: (b, c, v_blk)),
            pl.BlockSpec((1, cl, wv), lambda b, c: (b, c, v_blk + 1)),
            pl.BlockSpec((1, cl, 2 * nh), lambda b, c: (b, c, 0)),
            pl.BlockSpec((1, 1, 2 * nh, cl), lambda b, c: (b, c, 0, 0)),
            pl.BlockSpec((1, 2 * nh), lambda b, c: (0, 0)),
            pl.BlockSpec((2 * nh, 1), lambda b, c: (0, 0)),
            pl.BlockSpec((1, wv), lambda b, c: (0, 0)),
        ],
        out_specs=[
            pl.BlockSpec((1, cl, wv), lambda b, c: (b, c, 0)),
            pl.BlockSpec((1, nh, dqk, dv), lambda b, c: (b, 0, 0, 0)),
            pl.BlockSpec((1, nh, dqk), lambda b, c: (b, 0, 0)),
            pl.BlockSpec((1, nh, LANE), lambda b, c: (b, 0, 0)),
        ],
        out_shape=[
            jax.ShapeDtypeStruct((bsz, s, wv), F32),
            jax.ShapeDtypeStruct((bsz, nh, dqk, dv), F32),
            jax.ShapeDtypeStruct((bsz, nh, dqk), F32),
            jax.ShapeDtypeStruct((bsz, nh, LANE), F32),
        ],
        scratch_shapes=[pltpu.VMEM((nh, dqk, dv), F32), pltpu.VMEM((nh, dqk), F32),
                        pltpu.VMEM((nh, LANE), F32)],
        compiler_params=_cparams("parallel", "arbitrary"),
        name="mlstm_prompt",
    )(z3, z3, z3, z3, gc, gr, bc, br, gmh)


def _mlstm_step_body(q_ref, k_ref, v_ref, zo_ref, gs_ref, bias_ref, gmh_ref, c_ref, n_ref, m_ref, *rest,
                     nh, dqk, dv):
    ym_ref, c_out, n_out, m_out = rest[-4:]
    c_ref, c_out = c_ref.at[0], c_out.at[0]
    gates = gs_ref[0] + bias_ref[...]
    eye = (lax.broadcasted_iota(jnp.int32, (dqk, dqk), 0)
           == lax.broadcasted_iota(jnp.int32, (dqk, dqk), 1))

    def as_column(row):
        return jnp.sum(jnp.where(eye, row, 0.0), axis=1, keepdims=True)

    for h in range(nh):
        i_pre = gates[:, h:h + 1]
        logf = _log_sigmoid(gates[:, nh + h:nh + h + 1])
        m_prev = m_ref[0, :, h:h + 1]
        inter = logf + m_prev
        m_t = jnp.maximum(inter, i_pre)
        a_prev = jnp.exp(inter - m_t)
        a_new = jnp.exp(i_pre - m_t)
        qh = q_ref[0, :, h * dqk:(h + 1) * dqk]
        kh = k_ref[0, :, h * dqk:(h + 1) * dqk] * a_new
        vh = v_ref[0, :, h * dv:(h + 1) * dv]
        c_new = a_prev * c_ref[0, h] + as_column(kh) * vh
        n_new = a_prev * n_ref[0, h:h + 1, :] + kh
        num = jnp.sum(as_column(qh) * c_new, axis=0, keepdims=True)
        den = jnp.sum(qh * n_new, axis=1, keepdims=True)
        hh = num / jnp.maximum(jnp.abs(den), jnp.exp(-m_t))
        hn = _rms(hh, gmh_ref[:, h * dv:(h + 1) * dv])
        ym_ref[0, :, h * dv:(h + 1) * dv] = jax.nn.sigmoid(zo_ref[0, :, h * dv:(h + 1) * dv]) * hn
        c_out[0, h] = c_new
        n_out[0, h:h + 1, :] = n_new
        m_out[0, :, h:h + 1] = m_t


def mlstm_step(z3, gs, bias, gmh, state_c, n0, m0, layer, c_all=None, *, nh, dqk, dv):
    db = z3.shape[0]
    wq, wv = nh * dqk, nh * dv
    v_blk = (2 * wq) // wv
    body = functools.partial(_mlstm_step_body, nh=nh, dqk=dqk, dv=dv)
    m3 = m0.reshape(db, 1, nh)
    c_spec = pl.BlockSpec((1, 1, nh, dqk, dv), lambda s: (layer, s, 0, 0, 0))
    in_specs = [
        pl.BlockSpec((1, 1, wq), lambda s: (s, 0, 0)),
        pl.BlockSpec((1, 1, wq), lambda s: (s, 0, 1)),
        pl.BlockSpec((1, 1, wv), lambda s: (s, 0, v_blk)),
        pl.BlockSpec((1, 1, wv), lambda s: (s, 0, v_blk + 1)),
        pl.BlockSpec((1, 1, 2 * nh), lambda s: (s, 0, 0)),
        pl.BlockSpec((1, 2 * nh), lambda s: (0, 0)),
        pl.BlockSpec((1, wv), lambda s: (0, 0)),
        c_spec,
        pl.BlockSpec((1, nh, dqk), lambda s: (s, 0, 0)),
        pl.BlockSpec((1, 1, nh), lambda s: (s, 0, 0)),
    ]
    args = [z3, z3, z3, z3, gs, bias, gmh, state_c, n0, m3]
    aliases = {}
    if c_all is not None:
        in_specs.append(pl.BlockSpec(memory_space=pl.ANY))
        args.append(c_all)
        aliases = {len(args) - 1: 1}
    return pl.pallas_call(
        body,
        grid=(db,),
        in_specs=in_specs,
        out_specs=[
            pl.BlockSpec((1, 1, wv), lambda s: (s, 0, 0)),
            c_spec,
            pl.BlockSpec((1, nh, dqk), lambda s: (s, 0, 0)),
            pl.BlockSpec((1, 1, nh), lambda s: (s, 0, 0)),
        ],
        out_shape=[
            jax.ShapeDtypeStruct((db, 1, wv), F32),
            jax.ShapeDtypeStruct(state_c.shape, F32),
            jax.ShapeDtypeStruct((db, nh, dqk), F32),
            jax.ShapeDtypeStruct((db, 1, nh), F32),
        ],
        input_output_aliases=aliases,
        compiler_params=_cparams("parallel"),
        name="mlstm_step",
    )(*args)


def _rope_half(xb, cs):
    t = xb * cs
    return t + pltpu.roll(t, 64, axis=1)


def _query_heads(cq_b, wq_ref, cs, gq_ref, h, nope, qk_head, low):
    w = 2 * LANE
    qf = _bdot(cq_b, wq_ref[:, h * w:(h + 1) * w])
    qn = qf[:, :nope]
    qp = jnp.where(low, _rope_half(qf[:, nope:], cs), 0.0)
    ssq = jnp.sum(qn * qn, axis=1, keepdims=True) + jnp.sum(qp * qp, axis=1, keepdims=True)
    r = lax.rsqrt(ssq / qk_head + EPS) * (qk_head ** -0.5)
    return qn * r * gq_ref[:, :nope], qp * r * gq_ref[:, nope:]


def _key_heads(ckv_b, wkv_ref, kpe, kpe_sq, gk_ref, h, nope, qk_head):
    w = 2 * LANE
    kvf = _bdot(ckv_b, wkv_ref[:, h * w:(h + 1) * w])
    kn = kvf[:, :nope]
    r = lax.rsqrt((jnp.sum(kn * kn, axis=1, keepdims=True) + kpe_sq) / qk_head + EPS)
    return kn * r * gk_ref[:, :nope], kpe * r * gk_ref[:, nope:], kvf[:, nope:]


def _mla_prep_body(zcq_ref, zckv_ref, zkr_ref, cs_ref, gcq_ref, gckv_ref, gq_ref, gk_ref,
                   wq_ref, wkv_ref, q_out, k_out, v_out, ckv_out, kpe_out, *, nheads, nope, rope):
    qk_head = nope + rope
    cs = cs_ref[...]
    tm = cs.shape[0]
    low = lax.broadcasted_iota(jnp.int32, (tm, LANE), 1) < rope
    cq_b = _rms(zcq_ref[0], gcq_ref[...]).astype(BF16)
    ckv = _rms(zckv_ref[0], gckv_ref[...])
    ckv_out[0] = ckv
    ckv_b = ckv.astype(BF16)
    kpe = jnp.where(low, _rope_half(zkr_ref[0], cs), 0.0)
    kpe_out[0] = kpe[:, :rope]
    kpe_sq = jnp.sum(kpe * kpe, axis=1, keepdims=True)
    for h in range(nheads):
        qn, qp = _query_heads(cq_b, wq_ref, cs, gq_ref, h, nope, qk_head, low)
        q_out[0, h, :, :nope] = qn.astype(BF16)
        q_out[0, h, :, nope:] = qp.astype(BF16)
        kn, kp, v = _key_heads(ckv_b, wkv_ref, kpe, kpe_sq, gk_ref, h, nope, qk_head)
        k_out[0, h, :, :nope] = kn.astype(BF16)
        k_out[0, h, :, nope:] = kp.astype(BF16)
        v_out[0, h] = v.astype(BF16)


def mla_prep(z3, cs, gcq, gckv, gq, gk, wq, wkv, *, col_cq, nheads, nope, rope, av, tm_pref=256):
    bsz, s, _ = z3.shape
    ql, kvl = wq.shape[0], wkv.shape[0]
    assert nope == LANE and 2 * rope == LANE and av == LANE and ql == kvl
    tm = _tile(s, tm_pref)
    cb = col_cq // ql
    kr_blk = (col_cq + ql + kvl) // LANE
    body = functools.partial(_mla_prep_body, nheads=nheads, nope=nope, rope=rope)
    hw = 2 * LANE
    return pl.pallas_call(
        body,
        grid=(bsz, s // tm),
        in_specs=[
            pl.BlockSpec((1, tm, ql), lambda b, i: (b, i, cb)),
            pl.BlockSpec((1, tm, kvl), lambda b, i: (b, i, cb + 1)),
            pl.BlockSpec((1, tm, LANE), lambda b, i: (b, i, kr_blk)),
            pl.BlockSpec((tm, LANE), lambda b, i: (i, 0)),
            pl.BlockSpec((1, ql), lambda b, i: (0, 0)),
            pl.BlockSpec((1, kvl), lambda b, i: (0, 0)),
            pl.BlockSpec((1, hw), lambda b, i: (0, 0)),
            pl.BlockSpec((1, hw), lambda b, i: (0, 0)),
            pl.BlockSpec((ql, nheads * hw), lambda b, i: (0, 0)),
            pl.BlockSpec((kvl, nheads * hw), lambda b, i: (0, 0)),
        ],
        out_specs=[
            pl.BlockSpec((1, nheads, tm, hw), lambda b, i: (b, 0, i, 0)),
            pl.BlockSpec((1, nheads, tm, hw), lambda b, i: (b, 0, i, 0)),
            pl.BlockSpec((1, nheads, tm, av), lambda b, i: (b, 0, i, 0)),
            pl.BlockSpec((1, tm, kvl), lambda b, i: (b, i, 0)),
            pl.BlockSpec((1, tm, rope), lambda b, i: (b, i, 0)),
        ],
        out_shape=[
            jax.ShapeDtypeStruct((bsz, nheads, s, hw), BF16),
            jax.ShapeDtypeStruct((bsz, nheads, s, hw), BF16),
            jax.ShapeDtypeStruct((bsz, nheads, s, av), BF16),
            jax.ShapeDtypeStruct((bsz, s, kvl), F32),
            jax.ShapeDtypeStruct((bsz, s, rope), F32),
        ],
        compiler_params=_cparams("parallel", "parallel"),
        name="mla_prep",
    )(z3, z3, z3, cs, gcq, gckv, gq, gk, wq, wkv)


def _flash_body(q_ref, k_ref, v_ref, o_ref, s_scr, m_scr, l_scr, acc_scr, *, tq, tk):
    qi = pl.program_id(2)
    m_scr[...] = jnp.full_like(m_scr, -jnp.inf)
    l_scr[...] = jnp.zeros_like(l_scr)
    acc_scr[...] = jnp.zeros_like(acc_scr)
    q = q_ref[0, 0]

    def scores(ki, slot):
        off = pl.multiple_of(ki * tk, tk)
        s_scr[slot] = lax.dot_general(q, k_ref[0, 0, pl.ds(off, tk), :], _NT, preferred_element_type=F32)

    def update(ki, slot, masked):
        off = pl.multiple_of(ki * tk, tk)
        s = s_scr[slot]
        if masked:
            qpos = qi * tq + lax.broadcasted_iota(jnp.int32, (tq, tk), 0)
            kpos = off + lax.broadcasted_iota(jnp.int32, (tq, tk), 1)
            s = jnp.where(kpos <= qpos, s, -jnp.inf)
        m_old = m_scr[...]
        m_new = jnp.maximum(m_old, jnp.max(s, axis=1, keepdims=True))
        alpha = jnp.exp(m_old - m_new)
        p = jnp.exp(s - m_new)
        l_scr[...] = alpha * l_scr[...] + jnp.sum(p, axis=1, keepdims=True)
        acc_scr[...] = alpha * acc_scr[...] + _bdot(p.astype(BF16), v_ref[0, 0, pl.ds(off, tk), :])
        m_scr[...] = m_new

    scores(0, 0)

    def pair(i, carry):
        scores(2 * i + 1, 1)
        update(2 * i, 0, False)
        scores(2 * i + 2, 0)
        update(2 * i + 1, 1, False)
        return carry

    lax.fori_loop(0, qi, pair, 0)
    scores(2 * qi + 1, 1)
    update(2 * qi, 0, True)
    update(2 * qi + 1, 1, True)
    o_ref[0] = acc_scr[...] / l_scr[...]


def flash_attention(q, k, v, *, tq_pref=1024):
    bsz, nheads, s, hw = q.shape
    av = v.shape[-1]
    tq = _tile(s, tq_pref)
    tk = tq // 2
    body = functools.partial(_flash_body, tq=tq, tk=tk)
    return pl.pallas_call(
        body,
        grid=(bsz, nheads, s // tq),
        in_specs=[
            pl.BlockSpec((1, 1, tq, hw), lambda b, h, i: (b, h, i, 0)),
            pl.BlockSpec((1, 1, s, hw), lambda b, h, i: (b, h, 0, 0)),
            pl.BlockSpec((1, 1, s, av), lambda b, h, i: (b, h, 0, 0)),
        ],
        out_specs=pl.BlockSpec((1, tq, av), lambda b, h, i: (b, i, h)),
        out_shape=jax.ShapeDtypeStruct((bsz, s, nheads * av), F32),
        scratch_shapes=[pltpu.VMEM((2, tq, tk), F32), pltpu.VMEM((tq, 1), F32), pltpu.VMEM((tq, 1), F32),
                        pltpu.VMEM((tq, av), F32)],
        compiler_params=_cparams("parallel", "parallel", "arbitrary"),
        name="flash_attention",
    )(q, k, v)


def _sample_prep_body(zcq_ref, zckv_ref, zkr_ref, cs_ref, gcq_ref, gckv_ref, gq_ref, gk_ref,
                      wq_ref, wkv_ref, wukt_ref, qa_out, qp_out, self_out, v_out, ckv_out, kpe_out,
                      *, nheads, nope, rope):
    qk_head = nope + rope
    cs = cs_ref[...]
    tm = cs.shape[0]
    low = lax.broadcasted_iota(jnp.int32, (tm, LANE), 1) < rope
    cq_b = _rms(zcq_ref[...], gcq_ref[...]).astype(BF16)
    ckv = _rms(zckv_ref[...], gckv_ref[...])
    ckv_out[...] = ckv
    ckv_b = ckv.astype(BF16)
    kpe = jnp.where(low, _rope_half(zkr_ref[...], cs), 0.0)
    kpe_out[...] = kpe[:, :rope]
    kpe_sq = jnp.sum(kpe * kpe, axis=1, keepdims=True)
    for h in range(nheads):
        qn, qp = _query_heads(cq_b, wq_ref, cs, gq_ref, h, nope, qk_head, low)
        kn, kp, v = _key_heads(ckv_b, wkv_ref, kpe, kpe_sq, gk_ref, h, nope, qk_head)
        self_out[h] = jnp.broadcast_to(
            jnp.sum(qn * kn, axis=1, keepdims=True) + jnp.sum(qp * kp, axis=1, keepdims=True), (tm, LANE))
        v_out[h] = v
        qg = (qn * gk_ref[:, :nope]).astype(BF16)
        qa_out[h] = _bdot(qg, wukt_ref[h * nope:(h + 1) * nope, :]).astype(BF16)
        qp_out[h] = (qp * gk_ref[:, nope:])[:, :rope].astype(BF16)


def sample_prep(z, cs, gcq, gckv, gq, gk, wq, wkv, wukt, *, col_cq, nheads, nope, rope, av):
    db = z.shape[0]
    ql, kvl = wq.shape[0], wkv.shape[0]
    cb = col_cq // ql
    kr_blk = (col_cq + ql + kvl) // LANE
    hw = 2 * LANE
    body = functools.partial(_sample_prep_body, nheads=nheads, nope=nope, rope=rope)
    full = lambda shape: pl.BlockSpec(shape, lambda i: (0,) * len(shape))
    return pl.pallas_call(
        body,
        grid=(1,),
        in_specs=[
            pl.BlockSpec((db, ql), lambda i: (0, cb)),
            pl.BlockSpec((db, kvl), lambda i: (0, cb + 1)),
            pl.BlockSpec((db, LANE), lambda i: (0, kr_blk)),
            full((db, LANE)), full((1, ql)), full((1, kvl)), full((1, hw)), full((1, hw)),
            full((ql, nheads * hw)), full((kvl, nheads * hw)), full((nheads * nope, kvl)),
        ],
        out_specs=[
            full((nheads, db, kvl)), full((nheads, db, rope)), full((nheads, db, LANE)),
            full((nheads, db, av)), full((db, kvl)), full((db, rope)),
        ],
        out_shape=[
            jax.ShapeDtypeStruct((nheads, db, kvl), BF16),
            jax.ShapeDtypeStruct((nheads, db, rope), BF16),
            jax.ShapeDtypeStruct((nheads, db, LANE), F32),
            jax.ShapeDtypeStruct((nheads, db, av), F32),
            jax.ShapeDtypeStruct((db, kvl), F32),
            jax.ShapeDtypeStruct((db, rope), F32),
        ],
        compiler_params=_cparams("arbitrary"),
        name="sample_prep",
    )(z, z, z, cs, gcq, gckv, gq, gk, wq, wkv, wukt)


def _decode_body(pt_ref, qa_ref, qp_ref, wukt_ref, *refs, npg, nheads, nope, rope, page, sub):
    ckv_refs = refs[:npg]
    kpe_refs = refs[npg:2 * npg]
    acc_out, m_out, l_out = refs[2 * npg:2 * npg + 3]
    lhs_scr, cb_scr, lg_scr, m_scr, l_scr, acc_scr = refs[2 * npg + 3:]
    del pt_ref
    j = pl.program_id(1)
    qk_head = nope + rope
    nk = nheads * nope
    ppt = sub // page

    @pl.when(j == 0)
    def _():
        lhs_scr[0:nk, :] = wukt_ref[...]
        lhs_scr[nk:nk + nheads, :] = qa_ref[0]
        m_scr[...] = jnp.full_like(m_scr, -jnp.inf)
        l_scr[...] = jnp.zeros_like(l_scr)
        acc_scr[...] = jnp.zeros_like(acc_scr)

    qp = qp_ref[0]
    for t in range(npg // ppt):
        pages = range(t * ppt, (t + 1) * ppt)
        cb = jnp.concatenate([ckv_refs[p][0, 0].astype(BF16) for p in pages], axis=0)
        cb_scr[t * sub:(t + 1) * sub, :] = cb
        kpt = jnp.concatenate([kpe_refs[p][0, 0] for p in pages], axis=1)
        r = lax.dot_general(lhs_scr[...], cb, _NT, preferred_element_type=F32)
        kn = r[0:nk, :].reshape(nheads, nope, sub)
        ssq = jnp.sum(kn * kn, axis=1)
        pe_sq = jnp.sum(kpt * kpt, axis=0, keepdims=True)
        s_pe = _bdot(qp, kpt.astype(BF16))
        lg_scr[:, t * sub:(t + 1) * sub] = (r[nk:nk + nheads, :] + s_pe) * lax.rsqrt(
            (ssq + pe_sq) / qk_head + EPS)

    logits = lg_scr[...]
    m_old = m_scr[...]
    m_new = jnp.maximum(m_old, jnp.max(logits, axis=1, keepdims=True))
    alpha = jnp.exp(m_old - m_new)
    pr = jnp.exp(logits - m_new)
    l_scr[...] = alpha * l_scr[...] + jnp.sum(pr, axis=1, keepdims=True)
    acc_scr[...] = alpha * acc_scr[...] + _bdot(pr.astype(BF16), cb_scr[...])
    m_scr[...] = m_new

    @pl.when(j == pl.num_programs(1) - 1)
    def _():
        acc_out[0] = acc_scr[...]
        m_out[0] = jnp.broadcast_to(m_scr[...], (nheads, LANE))
        l_out[0] = jnp.broadcast_to(l_scr[...], (nheads, LANE))


def paged_decode(pt_flat, qa, qp, wukt, cache_ckv, cache_kpe_t, layer, *, n_pages, nheads, nope, rope,
                 npg=16, sub=512):
    db = qa.shape[0]
    page, kvl = cache_ckv.shape[2], cache_ckv.shape[3]
    npg = math.gcd(npg, n_pages)
    sub = math.gcd(sub, npg * page)
    nk = nheads * nope
    body = functools.partial(_decode_body, npg=npg, nheads=nheads, nope=nope, rope=rope, page=page, sub=sub)

    def page_spec(shape, p):
        return pl.BlockSpec((1, 1) + shape, lambda s, j, pt: (layer, pt[s * n_pages + j * npg + p], 0, 0))

    in_specs = [
        pl.BlockSpec((1, nheads, kvl), lambda s, j, pt: (s, 0, 0)),
        pl.BlockSpec((1, nheads, rope), lambda s, j, pt: (s, 0, 0)),
        pl.BlockSpec((nk, kvl), lambda s, j, pt: (0, 0)),
    ]
    in_specs += [page_spec((page, kvl), p) for p in range(npg)]
    in_specs += [page_spec((rope, page), p) for p in range(npg)]
    out_spec = lambda w: pl.BlockSpec((1, nheads, w), lambda s, j, pt: (s, 0, 0))
    grid_spec = pltpu.PrefetchScalarGridSpec(
        num_scalar_prefetch=1,
        grid=(db, n_pages // npg),
        in_specs=in_specs,
        out_specs=[out_spec(kvl), out_spec(LANE), out_spec(LANE)],
        scratch_shapes=[
            pltpu.VMEM((nk + nheads, kvl), BF16),
            pltpu.VMEM((npg * page, kvl), BF16),
            pltpu.VMEM((nheads, npg * page), F32),
            pltpu.VMEM((nheads, 1), F32),
            pltpu.VMEM((nheads, 1), F32),
            pltpu.VMEM((nheads, kvl), F32),
        ],
    )
    return pl.pallas_call(
        body,
        grid_spec=grid_spec,
        out_shape=[
            jax.ShapeDtypeStruct((db, nheads, kvl), F32),
            jax.ShapeDtypeStruct((db, nheads, LANE), F32),
            jax.ShapeDtypeStruct((db, nheads, LANE), F32),
        ],
        compiler_params=_cparams("arbitrary", "arbitrary"),
        name="paged_decode",
    )(pt_flat, qa, qp, wukt, *([cache_ckv] * npg), *([cache_kpe_t] * npg))


def _decode_finish_body(acc_ref, m_ref, l_ref, self_ref, v_ref, wuv_ref, o_ref, *, nheads, av):
    for h in range(nheads):
        m_old = m_ref[h][:, 0:1]
        ls = self_ref[h][:, 0:1]
        m_new = jnp.maximum(m_old, ls)
        alpha = jnp.exp(m_old - m_new)
        ps = jnp.exp(ls - m_new)
        denom = alpha * l_ref[h][:, 0:1] + ps
        ctx = (acc_ref[h] * alpha).astype(BF16)
        o_ref[:, h * av:(h + 1) * av] = (_bdot(ctx, wuv_ref[h]) + ps * v_ref[h]) / denom


def decode_finish(acc, m, l, self_logit, v_self, wuv):
    nheads, db, kvl = acc.shape
    av = wuv.shape[-1]
    full = lambda shape: pl.BlockSpec(shape, lambda i: (0,) * len(shape))
    return pl.pallas_call(
        functools.partial(_decode_finish_body, nheads=nheads, av=av),
        grid=(1,),
        in_specs=[full(acc.shape), full(m.shape), full(l.shape), full(self_logit.shape),
                  full(v_self.shape), full(wuv.shape)],
        out_specs=full((db, nheads * av)),
        out_shape=jax.ShapeDtypeStruct((db, nheads * av), F32),
        compiler_params=_cparams("arbitrary"),
        name="decode_finish",
    )(acc, m, l, self_logit, v_self, wuv)


def _rot_cols(w):
    half = w.shape[-1] // 2
    return jnp.concatenate([-w[..., half:], w[..., :half]], axis=-1)


def _pad_cols(w, n):
    return jnp.pad(w, ((0, 0), (0, n - w.shape[1])))


def _layer_weights(l, dims, norm_g, w_ffn_up, w_ffn_down, w_in, b_if, g_mh, g_cq, w_uq, g_ckv, w_ukv,
                   g_q, g_k, w_gate, b_gate, w_out, w_ple, w_pg):
    d, dff, fp = dims["d"], dims["dff"], dims["fp"]
    nh, dqk, dv = dims["nh"], dims["dqk"], dims["dv"]
    ah, ql, kvl, nope, rope, av = dims["ah"], dims["ql"], dims["kvl"], dims["nope"], dims["rope"], dims["av"]
    w = {}
    w["g"] = norm_g[l].reshape(4, 1, d)
    for i in range(2):
        up, down = w_ffn_up[l, i], w_ffn_down[l, i]
        w[f"wa{i}"] = _pad_cols(up[:, :dff], fp).astype(BF16)
        w[f"wb{i}"] = _pad_cols(up[:, dff:], fp).astype(BF16)
        w[f"wd{i}"] = jnp.pad(down, ((0, fp - dff), (0, 0))).astype(BF16)
    wi = w_in[l]
    o = [0]
    for width in (nh * dqk, nh * dqk, nh * dv, d, nh, nh, ql, kvl, rope):
        o.append(o[-1] + width)
    zq, zk, zv, zo, zi, zf, zcq, zckv, zkr = (wi[:, o[i]:o[i + 1]] for i in range(9))
    cols = [zq, zk * (dqk ** -0.5), zv, zo, zcq, zckv, zkr, _rot_cols(zkr), zi, zf]
    win = jnp.concatenate(cols, axis=1)
    w["win"] = _pad_cols(win, dims["nin"]).astype(BF16)
    w["b_col"] = b_if[l].reshape(1, 2 * nh)
    w["b_row"] = b_if[l].reshape(2 * nh, 1)
    w["gmh"] = g_mh[l].reshape(1, nh * dv)
    w["gcq"] = g_cq[l].reshape(1, ql)
    w["gckv"] = g_ckv[l].reshape(1, kvl)
    zpad = jnp.zeros((rope,), F32)
    w["gq"] = jnp.concatenate([g_q[l], zpad]).reshape(1, 2 * LANE)
    w["gk"] = jnp.concatenate([g_k[l], zpad]).reshape(1, 2 * LANE)
    uq = w_uq[l]
    pe = uq[:, :, nope:]
    w["wq"] = jnp.concatenate([uq[:, :, :nope], pe, _rot_cols(pe)], axis=-1).reshape(ql, ah * 2 * LANE).astype(BF16)
    ukv = w_ukv[l]
    w["wkv"] = ukv.reshape(kvl, ah * (nope + av)).astype(BF16)
    w["wukt"] = ukv[:, :, :nope].reshape(kvl, ah * nope).T.astype(BF16)
    w["wuv"] = jnp.transpose(ukv[:, :, nope:], (1, 0, 2)).astype(BF16)
    w["wgm"] = w_gate[l][:, :d].astype(BF16)
    w["wga"] = w_gate[l][:, d:].astype(BF16)
    w["bgm"] = b_gate[l][:d].reshape(1, d)
    w["bga"] = b_gate[l][d:].reshape(1, d)
    w["wo"] = w_out[l].astype(BF16)
    w["wple"] = w_ple[l].astype(BF16)
    w["wpg"] = w_pg[l].astype(BF16)
    return w


def _rope_table(pos, rope):
    half = rope // 2
    inv = ROPE_THETA ** (-jnp.arange(half, dtype=F32) / half)
    ang = pos.astype(F32)[:, None] * inv[None, :]
    cos, sin = jnp.cos(ang), jnp.sin(ang)
    return jnp.concatenate([cos, cos, sin, sin], axis=1)


def kernel(x_prompt, x_sample, cache_ckv, cache_kpe, state_C, state_n, state_m, page_table,
           p_prompt, p_sample, norm_g, w_ffn_up, w_ffn_down, w_in, b_if, g_mh, g_cq, w_uq,
           g_ckv, w_ukv, g_q, g_k, w_gate, b_gate, w_out, w_ple, w_pg):
    bsz, seq, d = x_prompt.shape
    db, dseq, _ = x_sample.shape
    assert dseq == 1
    depth, _, nh, dqk, dv = state_C.shape
    _, ql, ah, qk_head = w_uq.shape
    kvl = w_ukv.shape[1]
    rope = cache_kpe.shape[-1]
    nope = qk_head - rope
    av = w_ukv.shape[-1] - nope
    dff = w_ffn_down.shape[2]
    page = cache_ckv.shape[2]
    n_pages = page_table.shape[1]
    past = n_pages * page
    pd = p_prompt.shape[-1]
    col_cq = 2 * nh * dqk + nh * dv + d
    n_used = col_cq + ql + kvl + 2 * rope + 2 * nh
    col_gate = col_cq + ql + kvl + 2 * rope
    tn_in = 1536 if d >= 1536 else 256
    dims = dict(d=d, dff=dff, fp=-(-dff // 512) * 512 if dff > 512 else dff, nh=nh, dqk=dqk, dv=dv, ah=ah,
                ql=ql, kvl=kvl, nope=nope, rope=rope, av=av, nin=-(-n_used // tn_in) * tn_in)
    cl = _tile(seq, 256)
    nc = seq // cl

    cs_p = _rope_table(jnp.arange(seq), rope)
    cs_s = jnp.broadcast_to(_rope_table(past + jnp.arange(1), rope), (db, 2 * rope))
    pt_flat = page_table.reshape(-1).astype(jnp.int32)
    cache_kpe_t = jnp.swapaxes(cache_kpe, 2, 3)

    xp = x_prompt.reshape(bsz * seq, d)
    xs = x_sample.reshape(db, d)
    pp = p_prompt.reshape(depth, bsz * seq, pd)
    ps = p_sample.reshape(depth, db, pd)
    outs_p, outs_s = [], []
    c_all = None
    for l in range(depth):
        w = _layer_weights(l, dims, norm_g, w_ffn_up, w_ffn_down, w_in, b_if, g_mh, g_cq, w_uq, g_ckv,
                           w_ukv, g_q, g_k, w_gate, b_gate, w_out, w_ple, w_pg)
        mla = dict(col_cq=col_cq, nheads=ah, nope=nope, rope=rope, av=av)

        xp = ffn(xp, w["g"][0], w["wa0"], w["wb0"], w["wd0"])
        z = in_proj(xp, w["g"][1], w["win"], tn_pref=tn_in)
        z3 = z.reshape(bsz, seq, -1)
        gates = z3[:, :, col_gate:col_gate + 2 * nh]
        gates_r = jnp.swapaxes(gates.reshape(bsz, nc, cl, 2 * nh), 2, 3)
        ym, c_p, n_p, m_p = mlstm_prompt(z3, gates, gates_r, w["b_col"], w["b_row"], w["gmh"],
                                         nh=nh, dqk=dqk, dv=dv, cl=cl)
        qh, kh, vh, ckv_p, kpe_p = mla_prep(z3, cs_p, w["gcq"], w["gckv"], w["gq"], w["gk"],
                                            w["wq"], w["wkv"], **mla)
        ya = flash_attention(qh, kh, vh)
        xp = merge(xp, w["g"][1], w["wgm"], w["wga"], w["bgm"], w["bga"],
                   ym.reshape(bsz * seq, d), ya.reshape(bsz * seq, d), w["wo"])
        xp = ffn(xp, w["g"][2], w["wa1"], w["wb1"], w["wd1"])
        xp = ple(xp, w["g"][3], w["wpg"], pp[l], w["wple"])
        outs_p.append((ckv_p, kpe_p, c_p, n_p, m_p[:, :, 0]))

        xs = ffn(xs, w["g"][0], w["wa0"], w["wb0"], w["wd0"])
        zs = in_proj(xs, w["g"][1], w["win"], tn_pref=tn_in)
        zs3 = zs.reshape(db, 1, -1)
        ym_s, c_all, n_s, m_s = mlstm_step(zs3, zs3[:, :, col_gate:col_gate + 2 * nh], w["b_col"], w["gmh"],
                                           state_C, state_n[l], state_m[l], l, c_all, nh=nh, dqk=dqk, dv=dv)
        qa, qp, self_logit, v_self, ckv_s, kpe_s = sample_prep(
            zs, cs_s, w["gcq"], w["gckv"], w["gq"], w["gk"], w["wq"], w["wkv"], w["wukt"], **mla)
        acc, m_run, l_run = paged_decode(pt_flat, jnp.swapaxes(qa, 0, 1), jnp.swapaxes(qp, 0, 1), w["wukt"],
                                         cache_ckv, cache_kpe_t, l, n_pages=n_pages, nheads=ah,
                                         nope=nope, rope=rope)
        ya_s = decode_finish(jnp.swapaxes(acc, 0, 1), jnp.swapaxes(m_run, 0, 1), jnp.swapaxes(l_run, 0, 1),
                             self_logit, v_self, w["wuv"])
        xs = merge(xs, w["g"][1], w["wgm"], w["wga"], w["bgm"], w["bga"], ym_s.reshape(db, d), ya_s, w["wo"])
        xs = ffn(xs, w["g"][2], w["wa1"], w["wb1"], w["wd1"])
        xs = ple(xs, w["g"][3], w["wpg"], ps[l], w["wple"])
        outs_s.append((ckv_s.reshape(db, 1, kvl), kpe_s.reshape(db, 1, rope), None, n_s, m_s.reshape(db, nh)))

    stack = lambda outs, i: jnp.stack([o[i] for o in outs])
    return (xp.reshape(bsz, seq, d), xs.reshape(db, 1, d),
            stack(outs_p, 0), stack(outs_p, 1), stack(outs_p, 2), stack(outs_p, 3), stack(outs_p, 4),
            stack(outs_s, 0), stack(outs_s, 1), c_all, stack(outs_s, 3), stack(outs_s, 4))
```

```python
import functools
import math

import jax
import jax.numpy as jnp
from jax import lax
from jax.experimental import pallas as pl
from jax.experimental.pallas import tpu as pltpu

F32 = jnp.float32
BF16 = jnp.bfloat16
EPS = 1e-6
ROPE_THETA = 10000.0
LANE = 128
VMEM_LIMIT_BYTES = 56 * 2**20
_NT = (((1,), (1,)), ((), ()))


def _cparams(*sem):
    return pltpu.CompilerParams(dimension_semantics=sem, vmem_limit_bytes=VMEM_LIMIT_BYTES)


def _rms(x, g):
    return x * lax.rsqrt(jnp.mean(x * x, axis=-1, keepdims=True) + EPS) * g


def _log_sigmoid(x):
    return jnp.minimum(x, 0.0) - jnp.log(1.0 + jnp.exp(-jnp.abs(x)))


def _bdot(a, b):
    return jnp.dot(a, b, preferred_element_type=F32)


def _tile(n, pref):
    if n <= pref:
        return n
    t = pref
    while n % t:
        t -= 8
    return t


def _ffn_body(x_ref, g_ref, wa_ref, wb_ref, wd_ref, *rest, with_ple):
    o_ref, xn_ref, acc_ref = rest[-3:]
    j = pl.program_id(1)

    @pl.when(j == 0)
    def _():
        xn_ref[...] = _rms(x_ref[...], g_ref[...]).astype(BF16)
        acc_ref[...] = jnp.zeros_like(acc_ref)

    xn = xn_ref[...]
    a = _bdot(xn, wa_ref[...])
    b = _bdot(xn, wb_ref[...])
    h = (a * jax.nn.sigmoid(a) * b).astype(BF16)
    acc_ref[...] += _bdot(h, wd_ref[...])

    @pl.when(j == pl.num_programs(1) - 1)
    def _():
        y = x_ref[...] + 0.5 * acc_ref[...]
        if with_ple:
            g2_ref, wpg_ref, p_ref, wple_ref = rest[:4]
            gate = jax.nn.sigmoid(_bdot(_rms(y, g2_ref[...]).astype(BF16), wpg_ref[...]))
            y = y + gate * _bdot(p_ref[...].astype(BF16), wple_ref[...])
        o_ref[...] = y


def ffn(x, g, wa, wb, wd, ple=None, *, tm_pref=512, tf_pref=512):
    m, d = x.shape
    fp = wa.shape[1]
    tm, tf = _tile(m, tm_pref), _tile(fp, tf_pref)
    in_specs = [
        pl.BlockSpec((tm, d), lambda i, j: (i, 0)),
        pl.BlockSpec((1, d), lambda i, j: (0, 0)),
        pl.BlockSpec((d, tf), lambda i, j: (0, j)),
        pl.BlockSpec((d, tf), lambda i, j: (0, j)),
        pl.BlockSpec((tf, d), lambda i, j: (j, 0)),
    ]
    args = [x, g, wa, wb, wd]
    if ple is not None:
        g2, wpg, p, wple = ple
        pd = p.shape[1]
        in_specs += [
            pl.BlockSpec((1, d), lambda i, j: (0, 0)),
            pl.BlockSpec((d, d), lambda i, j: (0, 0), pipeline_mode=pl.Buffered(1)),
            pl.BlockSpec((tm, pd), lambda i, j: (i, 0)),
            pl.BlockSpec((pd, d), lambda i, j: (0, 0), pipeline_mode=pl.Buffered(1)),
        ]
        args += [g2, wpg, p, wple]
    return pl.pallas_call(
        functools.partial(_ffn_body, with_ple=ple is not None),
        grid=(m // tm, fp // tf),
        in_specs=in_specs,
        out_specs=pl.BlockSpec((tm, d), lambda i, j: (i, 0)),
        out_shape=jax.ShapeDtypeStruct((m, d), F32),
        scratch_shapes=[pltpu.VMEM((tm, d), BF16), pltpu.VMEM((tm, d), F32)],
        compiler_params=_cparams("parallel", "arbitrary"),
        name="ffn_ple" if ple is not None else "ffn",
    )(*args)


def _in_proj_body(x_ref, g_ref, w_ref, o_ref, xn_ref):
    @pl.when(pl.program_id(1) == 0)
    def _():
        xn_ref[...] = _rms(x_ref[...], g_ref[...]).astype(BF16)

    o_ref[...] = _bdot(xn_ref[...], w_ref[...])


def in_proj(x, g, w, *, tm_pref=512, tn_pref=1536):
    m, d = x.shape
    n = w.shape[1]
    tm, tn = _tile(m, tm_pref), _tile(n, tn_pref)
    return pl.pallas_call(
        _in_proj_body,
        grid=(m // tm, n // tn),
        in_specs=[
            pl.BlockSpec((tm, d), lambda i, j: (i, 0)),
            pl.BlockSpec((1, d), lambda i, j: (0, 0)),
            pl.BlockSpec((d, tn), lambda i, j: (0, j)),
        ],
        out_specs=pl.BlockSpec((tm, tn), lambda i, j: (i, j)),
        out_shape=jax.ShapeDtypeStruct((m, n), F32),
        scratch_shapes=[pltpu.VMEM((tm, d), BF16)],
        compiler_params=_cparams("parallel", "arbitrary"),
        name="in_proj",
    )(x, g, w)


def _merge_body(x_ref, g_ref, wgm_ref, wga_ref, bm_ref, ba_ref, ym_ref, ya_ref, wo_ref,
                o_ref, xn_ref, mix_ref, *, tn):
    j = pl.program_id(1)

    @pl.when(j == 0)
    def _():
        xn_ref[...] = _rms(x_ref[...], g_ref[...]).astype(BF16)

    xn = xn_ref[...]
    gm = jax.nn.sigmoid(_bdot(xn, wgm_ref[...]) + bm_ref[...])
    ga = jax.nn.sigmoid(_bdot(xn, wga_ref[...]) + ba_ref[...])
    col = pl.multiple_of(j * tn, LANE)
    mix_ref[:, pl.ds(col, tn)] = (gm * ym_ref[...] + ga * ya_ref[...]).astype(BF16)

    @pl.when(j == pl.num_programs(1) - 1)
    def _():
        o_ref[...] = x_ref[...] + _bdot(mix_ref[...], wo_ref[...])


def merge(x, g, wgm, wga, bm, ba, ym, ya, wo, *, tm_pref=512, tn_pref=512):
    m, d = x.shape
    tm, tn = _tile(m, tm_pref), _tile(d, tn_pref)
    return pl.pallas_call(
        functools.partial(_merge_body, tn=tn),
        grid=(m // tm, d // tn),
        in_specs=[
            pl.BlockSpec((tm, d), lambda i, j: (i, 0)),
            pl.BlockSpec((1, d), lambda i, j: (0, 0)),
            pl.BlockSpec((d, tn), lambda i, j: (0, j)),
            pl.BlockSpec((d, tn), lambda i, j: (0, j)),
            pl.BlockSpec((1, tn), lambda i, j: (0, j)),
            pl.BlockSpec((1, tn), lambda i, j: (0, j)),
            pl.BlockSpec((tm, tn), lambda i, j: (i, j)),
            pl.BlockSpec((tm, tn), lambda i, j: (i, j)),
            pl.BlockSpec((d, d), lambda i, j: (0, 0), pipeline_mode=pl.Buffered(1)),
        ],
        out_specs=pl.BlockSpec((tm, d), lambda i, j: (i, 0)),
        out_shape=jax.ShapeDtypeStruct((m, d), F32),
        scratch_shapes=[pltpu.VMEM((tm, d), BF16), pltpu.VMEM((tm, d), BF16)],
        compiler_params=_cparams("parallel", "arbitrary"),
        name="merge",
    )(x, g, wgm, wga, bm, ba, ym, ya, wo)


def _mlstm_chunk_body(q_ref, k_ref, v_ref, zo_ref, gc_ref, gr_ref, bc_ref, br_ref, gmh_ref,
                      ym_ref, c_out, n_out, m_out, c_scr, n_scr, m_scr, *, nh, dqk, dv, cl):
    c = pl.program_id(1)

    @pl.when(c == 0)
    def _():
        c_scr[...] = jnp.zeros_like(c_scr)
        n_scr[...] = jnp.zeros_like(n_scr)
        m_scr[...] = jnp.zeros_like(m_scr)

    gc = gc_ref[0] + bc_ref[...]
    gr = gr_ref[0, 0] + br_ref[...]
    t_idx = lax.broadcasted_iota(jnp.int32, (cl, cl), 0)
    j_idx = lax.broadcasted_iota(jnp.int32, (cl, cl), 1)
    causal = j_idx <= t_idx
    for h in range(nh):
        i_col = gc[:, h:h + 1]
        f_col = _log_sigmoid(gc[:, nh + h:nh + h + 1])
        i_row = gr[h:h + 1, :]
        f_row = _log_sigmoid(gr[nh + h:nh + h + 1, :])
        b_col = jnp.sum(jnp.where(causal, f_row, 0.0), axis=1, keepdims=True)
        b_row = jnp.sum(jnp.where(t_idx <= j_idx, f_col, 0.0), axis=0, keepdims=True)
        m_prev = m_scr[h:h + 1, 0:1]
        d = jnp.where(causal, b_col - b_row + i_row, -jnp.inf)
        inter = b_col + m_prev
        m_t = jnp.maximum(inter, jnp.max(d, axis=1, keepdims=True))
        w_inter = jnp.exp(inter - m_t)
        qh = q_ref[0, :, h * dqk:(h + 1) * dqk]
        kh = k_ref[0, :, h * dqk:(h + 1) * dqk]
        vb = v_ref[0, :, h * dv:(h + 1) * dv].astype(BF16)
        qb = qh.astype(BF16)
        s = lax.dot_general(qb, kh.astype(BF16), _NT, preferred_element_type=F32) * jnp.exp(d - m_t)
        c_old = c_scr[h]
        n_old = n_scr[h:h + 1, :]
        num = w_inter * _bdot(qb, c_old.astype(BF16)) + _bdot(s.astype(BF16), vb)
        den = w_inter * jnp.sum(qh * n_old, axis=1, keepdims=True) + jnp.sum(s, axis=1, keepdims=True)
        hh = num / jnp.maximum(jnp.abs(den), jnp.exp(-m_t))
        hn = _rms(hh, gmh_ref[:, h * dv:(h + 1) * dv])
        ym_ref[0, :, h * dv:(h + 1) * dv] = jax.nn.sigmoid(zo_ref[0, :, h * dv:(h + 1) * dv]) * hn
        b_last = b_col[cl - 1:cl, :]
        g_col = b_last - b_col + i_col
        g_row = b_last - b_row + i_row
        m_new = jnp.maximum(b_last + m_prev, jnp.max(g_row, axis=1, keepdims=True))
        a_prev = jnp.exp(b_last + m_prev - m_new)
        ka = kh * jnp.exp(g_col - m_new)
        c_scr[h] = a_prev * c_old + _bdot(ka.T.astype(BF16), vb)
        n_scr[h:h + 1, :] = a_prev * n_old + jnp.sum(ka, axis=0, keepdims=True)
        m_scr[h:h + 1, :] = jnp.broadcast_to(m_new, (1, LANE))

    @pl.when(c == pl.num_programs(1) - 1)
    def _():
        c_out[0] = c_scr[...]
        n_out[0] = n_scr[...]
        m_out[0] = m_scr[...]


def mlstm_prompt(z3, gc, gr, bc, br, gmh, *, nh, dqk, dv, cl):
    bsz, s, _ = z3.shape
    nc = s // cl
    wq, wv = nh * dqk, nh * dv
    assert wq % LANE == 0 and wv % wq == 0
    kq, kv = wq // wq, wv // wq
    v_blk = (2 * wq) // wv
    assert (2 * wq) % wv == 0
    body = functools.partial(_mlstm_chunk_body, nh=nh, dqk=dqk, dv=dv, cl=cl)
    return pl.pallas_call(
        body,
        grid=(bsz, nc),
        in_specs=[
            pl.BlockSpec((1, cl, wq), lambda b, c: (b, c, 0)),
            pl.BlockSpec((1, cl, wq), lambda b, c: (b, c, kq)),
            pl.BlockSpec((1, cl, wv), lambda b, c: (b, c, v_blk)),
            pl.BlockSpec((1, cl, wv), lambda b, c: (b, c, v_blk + 1)),
            pl.BlockSpec((1, cl, 2 * nh), lambda b, c: (b, c, 0)),
            pl.BlockSpec((1, 1, 2 * nh, cl), lambda b, c: (b, c, 0, 0)),
            pl.BlockSpec((1, 2 * nh), lambda b, c: (0, 0)),
            pl.BlockSpec((2 * nh, 1), lambda b, c: (0, 0)),
            pl.BlockSpec((1, wv), lambda b, c: (0, 0)),
        ],
        out_specs=[
            pl.BlockSpec((1, cl, wv), lambda b, c: (b, c, 0)),
            pl.BlockSpec((1, nh, dqk, dv), lambda b, c: (b, 0, 0, 0)),
            pl.BlockSpec((1, nh, dqk), lambda b, c: (b, 0, 0)),
            pl.BlockSpec((1, nh, LANE), lambda b, c: (b, 0, 0)),
        ],
        out_shape=[
            jax.ShapeDtypeStruct((bsz, s, wv), F32),
            jax.ShapeDtypeStruct((bsz, nh, dqk, dv), F32),
            jax.ShapeDtypeStruct((bsz, nh, dqk), F32),
            jax.ShapeDtypeStruct((bsz, nh, LANE), F32),
        ],
        scratch_shapes=[pltpu.VMEM((nh, dqk, dv), F32), pltpu.VMEM((nh, dqk), F32),
                        pltpu.VMEM((nh, LANE), F32)],
        compiler_params=_cparams("parallel", "arbitrary"),
        name="mlstm_prompt",
    )(z3, z3, z3, z3, gc, gr, bc, br, gmh)


def _mlstm_step_body(q_ref, k_ref, v_ref, zo_ref, gs_ref, bias_ref, gmh_ref, c_ref, n_ref, m_ref, *rest,
                     nh, dqk, dv):
    ym_ref, c_out, n_out, m_out = rest[-4:]
    c_ref, c_out = c_ref.at[0], c_out.at[0]
    gates = gs_ref[0] + bias_ref[...]
    eye = (lax.broadcasted_iota(jnp.int32, (dqk, dqk), 0)
           == lax.broadcasted_iota(jnp.int32, (dqk, dqk), 1))

    def as_column(row):
        return jnp.sum(jnp.where(eye, row, 0.0), axis=1, keepdims=True)

    for h in range(nh):
        i_pre = gates[:, h:h + 1]
        logf = _log_sigmoid(gates[:, nh + h:nh + h + 1])
        m_prev = m_ref[0, :, h:h + 1]
        inter = logf + m_prev
        m_t = jnp.maximum(inter, i_pre)
        a_prev = jnp.exp(inter - m_t)
        a_new = jnp.exp(i_pre - m_t)
        qh = q_ref[0, :, h * dqk:(h + 1) * dqk]
        kh = k_ref[0, :, h * dqk:(h + 1) * dqk] * a_new
        vh = v_ref[0, :, h * dv:(h + 1) * dv]
        c_new = a_prev * c_ref[0, h] + as_column(kh) * vh
        n_new = a_prev * n_ref[0, h:h + 1, :] + kh
        num = jnp.sum(as_column(qh) * c_new, axis=0, keepdims=True)
        den = jnp.sum(qh * n_new, axis=1, keepdims=True)
        hh = num / jnp.maximum(jnp.abs(den), jnp.exp(-m_t))
        hn = _rms(hh, gmh_ref[:, h * dv:(h + 1) * dv])
        ym_ref[0, :, h * dv:(h + 1) * dv] = jax.nn.sigmoid(zo_ref[0, :, h * dv:(h + 1) * dv]) * hn
        c_out[0, h] = c_new
        n_out[0, h:h + 1, :] = n_new
        m_out[0, :, h:h + 1] = m_t


def mlstm_step(z3, gs, bias, gmh, state_c, n0, m0, layer, c_all=None, *, nh, dqk, dv):
    db = z3.shape[0]
    wq, wv = nh * dqk, nh * dv
    v_blk = (2 * wq) // wv
    body = functools.partial(_mlstm_step_body, nh=nh, dqk=dqk, dv=dv)
    m3 = m0.reshape(db, 1, nh)
    c_spec = pl.BlockSpec((1, 1, nh, dqk, dv), lambda s: (layer, s, 0, 0, 0))
    in_specs = [
        pl.BlockSpec((1, 1, wq), lambda s: (s, 0, 0)),
        pl.BlockSpec((1, 1, wq), lambda s: (s, 0, 1)),
        pl.BlockSpec((1, 1, wv), lambda s: (s, 0, v_blk)),
        pl.BlockSpec((1, 1, wv), lambda s: (s, 0, v_blk + 1)),
        pl.BlockSpec((1, 1, 2 * nh), lambda s: (s, 0, 0)),
        pl.BlockSpec((1, 2 * nh), lambda s: (0, 0)),
        pl.BlockSpec((1, wv), lambda s: (0, 0)),
        c_spec,
        pl.BlockSpec((1, nh, dqk), lambda s: (s, 0, 0)),
        pl.BlockSpec((1, 1, nh), lambda s: (s, 0, 0)),
    ]
    args = [z3, z3, z3, z3, gs, bias, gmh, state_c, n0, m3]
    aliases = {}
    if c_all is not None:
        in_specs.append(pl.BlockSpec(memory_space=pl.ANY))
        args.append(c_all)
        aliases = {len(args) - 1: 1}
    return pl.pallas_call(
        body,
        grid=(db,),
        in_specs=in_specs,
        out_specs=[
            pl.BlockSpec((1, 1, wv), lambda s: (s, 0, 0)),
            c_spec,
            pl.BlockSpec((1, nh, dqk), lambda s: (s, 0, 0)),
            pl.BlockSpec((1, 1, nh), lambda s: (s, 0, 0)),
        ],
        out_shape=[
            jax.ShapeDtypeStruct((db, 1, wv), F32),
            jax.ShapeDtypeStruct(state_c.shape, F32),
            jax.ShapeDtypeStruct((db, nh, dqk), F32),
            jax.ShapeDtypeStruct((db, 1, nh), F32),
        ],
        input_output_aliases=aliases,
        compiler_params=_cparams("parallel"),
        name="mlstm_step",
    )(*args)


def _rope_half(xb, cs):
    t = xb * cs
    return t + pltpu.roll(t, 64, axis=1)


def _query_heads(cq_b, wq_ref, cs, gq_ref, h, nope, qk_head, low):
    w = 2 * LANE
    qf = _bdot(cq_b, wq_ref[:, h * w:(h + 1) * w])
    qn = qf[:, :nope]
    qp = jnp.where(low, _rope_half(qf[:, nope:], cs), 0.0)
    ssq = jnp.sum(qn * qn, axis=1, keepdims=True) + jnp.sum(qp * qp, axis=1, keepdims=True)
    r = lax.rsqrt(ssq / qk_head + EPS) * (qk_head ** -0.5)
    return qn * r * gq_ref[:, :nope], qp * r * gq_ref[:, nope:]


def _key_heads(ckv_b, wkv_ref, kpe, kpe_sq, gk_ref, h, nope, qk_head):
    w = 2 * LANE
    kvf = _bdot(ckv_b, wkv_ref[:, h * w:(h + 1) * w])
    kn = kvf[:, :nope]
    r = lax.rsqrt((jnp.sum(kn * kn, axis=1, keepdims=True) + kpe_sq) / qk_head + EPS)
    return kn * r * gk_ref[:, :nope], kpe * r * gk_ref[:, nope:], kvf[:, nope:]


def _mla_prep_body(zcq_ref, zckv_ref, zkr_ref, cs_ref, gcq_ref, gckv_ref, gq_ref, gk_ref,
                   wq_ref, wkv_ref, q_out, k_out, v_out, ckv_out, kpe_out, *, nheads, nope, rope):
    qk_head = nope + rope
    cs = cs_ref[...]
    tm = cs.shape[0]
    low = lax.broadcasted_iota(jnp.int32, (tm, LANE), 1) < rope
    cq_b = _rms(zcq_ref[0], gcq_ref[...]).astype(BF16)
    ckv = _rms(zckv_ref[0], gckv_ref[...])
    ckv_out[0] = ckv
    ckv_b = ckv.astype(BF16)
    kpe = jnp.where(low, _rope_half(zkr_ref[0], cs), 0.0)
    kpe_out[0] = kpe[:, :rope]
    kpe_sq = jnp.sum(kpe * kpe, axis=1, keepdims=True)
    for h in range(nheads):
        qn, qp = _query_heads(cq_b, wq_ref, cs, gq_ref, h, nope, qk_head, low)
        q_out[0, h, :, :nope] = qn.astype(BF16)
        q_out[0, h, :, nope:] = qp.astype(BF16)
        kn, kp, v = _key_heads(ckv_b, wkv_ref, kpe, kpe_sq, gk_ref, h, nope, qk_head)
        k_out[0, h, :, :nope] = kn.astype(BF16)
        k_out[0, h, :, nope:] = kp.astype(BF16)
        v_out[0, h] = v.astype(BF16)


def mla_prep(z3, cs, gcq, gckv, gq, gk, wq, wkv, *, col_cq, nheads, nope, rope, av, tm_pref=256):
    bsz, s, _ = z3.shape
    ql, kvl = wq.shape[0], wkv.shape[0]
    assert nope == LANE and 2 * rope == LANE and av == LANE and ql == kvl
    tm = _tile(s, tm_pref)
    cb = col_cq // ql
    kr_blk = (col_cq + ql + kvl) // LANE
    body = functools.partial(_mla_prep_body, nheads=nheads, nope=nope, rope=rope)
    hw = 2 * LANE
    return pl.pallas_call(
        body,
        grid=(bsz, s // tm),
        in_specs=[
            pl.BlockSpec((1, tm, ql), lambda b, i: (b, i, cb)),
            pl.BlockSpec((1, tm, kvl), lambda b, i: (b, i, cb + 1)),
            pl.BlockSpec((1, tm, LANE), lambda b, i: (b, i, kr_blk)),
            pl.BlockSpec((tm, LANE), lambda b, i: (i, 0)),
            pl.BlockSpec((1, ql), lambda b, i: (0, 0)),
            pl.BlockSpec((1, kvl), lambda b, i: (0, 0)),
            pl.BlockSpec((1, hw), lambda b, i: (0, 0)),
            pl.BlockSpec((1, hw), lambda b, i: (0, 0)),
            pl.BlockSpec((ql, nheads * hw), lambda b, i: (0, 0)),
            pl.BlockSpec((kvl, nheads * hw), lambda b, i: (0, 0)),
        ],
        out_specs=[
            pl.BlockSpec((1, nheads, tm, hw), lambda b, i: (b, 0, i, 0)),
            pl.BlockSpec((1, nheads, tm, hw), lambda b, i: (b, 0, i, 0)),
            pl.BlockSpec((1, nheads, tm, av), lambda b, i: (b, 0, i, 0)),
            pl.BlockSpec((1, tm, kvl), lambda b, i: (b, i, 0)),
            pl.BlockSpec((1, tm, rope), lambda b, i: (b, i, 0)),
        ],
        out_shape=[
            jax.ShapeDtypeStruct((bsz, nheads, s, hw), BF16),
            jax.ShapeDtypeStruct((bsz, nheads, s, hw), BF16),
            jax.ShapeDtypeStruct((bsz, nheads, s, av), BF16),
            jax.ShapeDtypeStruct((bsz, s, kvl), F32),
            jax.ShapeDtypeStruct((bsz, s, rope), F32),
        ],
        compiler_params=_cparams("parallel", "parallel"),
        name="mla_prep",
    )(z3, z3, z3, cs, gcq, gckv, gq, gk, wq, wkv)


def _flash_body(q_ref, k_ref, v_ref, o_ref, s_scr, m_scr, l_scr, acc_scr, *, tq, tk):
    qi = pl.program_id(2)
    m_scr[...] = jnp.full_like(m_scr, -jnp.inf)
    l_scr[...] = jnp.zeros_like(l_scr)
    acc_scr[...] = jnp.zeros_like(acc_scr)
    q = q_ref[0, 0]

    def scores(ki, slot, r0=0):
        off = pl.multiple_of(ki * tk, tk)
        s_scr[slot, r0:, :] = lax.dot_general(q[r0:], k_ref[0, 0, pl.ds(off, tk), :], _NT,
                                              preferred_element_type=F32)

    def update(ki, slot, r0=0, r1=tq, triangular=False):
        off = pl.multiple_of(ki * tk, tk)
        s = s_scr[slot, r0:r1, :]
        if triangular:
            row = lax.broadcasted_iota(jnp.int32, (tk, tk), 0)
            col = lax.broadcasted_iota(jnp.int32, (tk, tk), 1)
            s = jnp.where(col <= row, s, -jnp.inf)
        m_old = m_scr[r0:r1]
        m_new = jnp.maximum(m_old, jnp.max(s, axis=1, keepdims=True))
        alpha = jnp.exp(m_old - m_new)
        p = jnp.exp(s - m_new)
        l_scr[r0:r1] = alpha * l_scr[r0:r1] + jnp.sum(p, axis=1, keepdims=True)
        acc_scr[r0:r1] = alpha * acc_scr[r0:r1] + _bdot(p.astype(BF16), v_ref[0, 0, pl.ds(off, tk), :])
        m_scr[r0:r1] = m_new

    scores(0, 0)

    def pair(i, carry):
        scores(2 * i + 1, 1)
        update(2 * i, 0)
        scores(2 * i + 2, 0)
        update(2 * i + 1, 1)
        return carry

    lax.fori_loop(0, qi, pair, 0)
    scores(2 * qi + 1, 1, tk)
    update(2 * qi, 0, 0, tk, triangular=True)
    update(2 * qi, 0, tk, tq)
    update(2 * qi + 1, 1, tk, tq, triangular=True)
    o_ref[0] = acc_scr[...] / l_scr[...]


def flash_attention(q, k, v, *, tq_pref=1024):
    bsz, nheads, s, hw = q.shape
    av = v.shape[-1]
    tq = _tile(s, tq_pref)
    tk = tq // 2
    body = functools.partial(_flash_body, tq=tq, tk=tk)
    return pl.pallas_call(
        body,
        grid=(bsz, nheads, s // tq),
        in_specs=[
            pl.BlockSpec((1, 1, tq, hw), lambda b, h, i: (b, h, i, 0)),
            pl.BlockSpec((1, 1, s, hw), lambda b, h, i: (b, h, 0, 0)),
            pl.BlockSpec((1, 1, s, av), lambda b, h, i: (b, h, 0, 0)),
        ],
        out_specs=pl.BlockSpec((1, tq, av), lambda b, h, i: (b, i, h)),
        out_shape=jax.ShapeDtypeStruct((bsz, s, nheads * av), F32),
        scratch_shapes=[pltpu.VMEM((2, tq, tk), F32), pltpu.VMEM((tq, 1), F32), pltpu.VMEM((tq, 1), F32),
                        pltpu.VMEM((tq, av), F32)],
        compiler_params=_cparams("parallel", "parallel", "arbitrary"),
        name="flash_attention",
    )(q, k, v)


def _sample_prep_body(zcq_ref, zckv_ref, zkr_ref, cs_ref, gcq_ref, gckv_ref, gq_ref, gk_ref,
                      wq_ref, wkv_ref, wukt_ref, qa_out, qp_out, self_out, v_out, ckv_out, kpe_out,
                      *, nheads, nope, rope):
    qk_head = nope + rope
    cs = cs_ref[...]
    tm = cs.shape[0]
    low = lax.broadcasted_iota(jnp.int32, (tm, LANE), 1) < rope
    cq_b = _rms(zcq_ref[...], gcq_ref[...]).astype(BF16)
    ckv = _rms(zckv_ref[...], gckv_ref[...])
    ckv_out[...] = ckv
    ckv_b = ckv.astype(BF16)
    kpe = jnp.where(low, _rope_half(zkr_ref[...], cs), 0.0)
    kpe_out[...] = kpe[:, :rope]
    kpe_sq = jnp.sum(kpe * kpe, axis=1, keepdims=True)
    for h in range(nheads):
        qn, qp = _query_heads(cq_b, wq_ref, cs, gq_ref, h, nope, qk_head, low)
        kn, kp, v = _key_heads(ckv_b, wkv_ref, kpe, kpe_sq, gk_ref, h, nope, qk_head)
        self_out[h] = jnp.broadcast_to(
            jnp.sum(qn * kn, axis=1, keepdims=True) + jnp.sum(qp * kp, axis=1, keepdims=True), (tm, LANE))
        v_out[h] = v
        qg = (qn * gk_ref[:, :nope]).astype(BF16)
        qa_out[h] = _bdot(qg, wukt_ref[h * nope:(h + 1) * nope, :]).astype(BF16)
        qp_out[h] = (qp * gk_ref[:, nope:])[:, :rope].astype(BF16)


def sample_prep(z, cs, gcq, gckv, gq, gk, wq, wkv, wukt, *, col_cq, nheads, nope, rope, av):
    db = z.shape[0]
    ql, kvl = wq.shape[0], wkv.shape[0]
    cb = col_cq // ql
    kr_blk = (col_cq + ql + kvl) // LANE
    hw = 2 * LANE
    body = functools.partial(_sample_prep_body, nheads=nheads, nope=nope, rope=rope)
    full = lambda shape: pl.BlockSpec(shape, lambda i: (0,) * len(shape))
    return pl.pallas_call(
        body,
        grid=(1,),
        in_specs=[
            pl.BlockSpec((db, ql), lambda i: (0, cb)),
            pl.BlockSpec((db, kvl), lambda i: (0, cb + 1)),
            pl.BlockSpec((db, LANE), lambda i: (0, kr_blk)),
            full((db, LANE)), full((1, ql)), full((1, kvl)), full((1, hw)), full((1, hw)),
            full((ql, nheads * hw)), full((kvl, nheads * hw)), full((nheads * nope, kvl)),
        ],
        out_specs=[
            full((nheads, db, kvl)), full((nheads, db, rope)), full((nheads, db, LANE)),
            full((nheads, db, av)), full((db, kvl)), full((db, rope)),
        ],
        out_shape=[
            jax.ShapeDtypeStruct((nheads, db, kvl), BF16),
            jax.ShapeDtypeStruct((nheads, db, rope), BF16),
            jax.ShapeDtypeStruct((nheads, db, LANE), F32),
            jax.ShapeDtypeStruct((nheads, db, av), F32),
            jax.ShapeDtypeStruct((db, kvl), F32),
            jax.ShapeDtypeStruct((db, rope), F32),
        ],
        compiler_params=_cparams("arbitrary"),
        name="sample_prep",
    )(z, z, z, cs, gcq, gckv, gq, gk, wq, wkv, wukt)


def _decode_body(pt_ref, qa_ref, qp_ref, wukt_ref, *refs, npg, nheads, nope, rope, page, sub):
    ckv_refs = refs[:npg]
    kpe_refs = refs[npg:2 * npg]
    acc_out, m_out, l_out = refs[2 * npg:2 * npg + 3]
    lhs_scr, cb_scr, lg_scr, m_scr, l_scr, acc_scr = refs[2 * npg + 3:]
    del pt_ref
    j = pl.program_id(1)
    qk_head = nope + rope
    nk = nheads * nope
    ppt = sub // page

    @pl.when(j == 0)
    def _():
        lhs_scr[0:nk, :] = wukt_ref[...]
        lhs_scr[nk:nk + nheads, :] = qa_ref[0]
        m_scr[...] = jnp.full_like(m_scr, -jnp.inf)
        l_scr[...] = jnp.zeros_like(l_scr)
        acc_scr[...] = jnp.zeros_like(acc_scr)

    qp = qp_ref[0]
    for t in range(npg // ppt):
        pages = range(t * ppt, (t + 1) * ppt)
        cb = jnp.concatenate([ckv_refs[p][0, 0].astype(BF16) for p in pages], axis=0)
        cb_scr[t * sub:(t + 1) * sub, :] = cb
        kpt = jnp.concatenate([kpe_refs[p][0, 0] for p in pages], axis=1)
        r = lax.dot_general(lhs_scr[...], cb, _NT, preferred_element_type=F32)
        kn = r[0:nk, :].reshape(nheads, nope, sub)
        ssq = jnp.sum(kn * kn, axis=1)
        pe_sq = jnp.sum(kpt * kpt, axis=0, keepdims=True)
        s_pe = _bdot(qp, kpt.astype(BF16))
        lg_scr[:, t * sub:(t + 1) * sub] = (r[nk:nk + nheads, :] + s_pe) * lax.rsqrt(
            (ssq + pe_sq) / qk_head + EPS)

    logits = lg_scr[...]
    m_old = m_scr[...]
    m_new = jnp.maximum(m_old, jnp.max(logits, axis=1, keepdims=True))
    alpha = jnp.exp(m_old - m_new)
    pr = jnp.exp(logits - m_new)
    l_scr[...] = alpha * l_scr[...] + jnp.sum(pr, axis=1, keepdims=True)
    acc_scr[...] = alpha * acc_scr[...] + _bdot(pr.astype(BF16), cb_scr[...])
    m_scr[...] = m_new

    @pl.when(j == pl.num_programs(1) - 1)
    def _():
        acc_out[0] = acc_scr[...]
        m_out[0] = jnp.broadcast_to(m_scr[...], (nheads, LANE))
        l_out[0] = jnp.broadcast_to(l_scr[...], (nheads, LANE))


def paged_decode(pt_flat, qa, qp, wukt, cache_ckv, cache_kpe_t, layer, *, n_pages, nheads, nope, rope,
                 npg=32, sub=512):
    db = qa.shape[0]
    page, kvl = cache_ckv.shape[2], cache_ckv.shape[3]
    npg = math.gcd(npg, n_pages)
    sub = math.gcd(sub, npg * page)
    nk = nheads * nope
    body = functools.partial(_decode_body, npg=npg, nheads=nheads, nope=nope, rope=rope, page=page, sub=sub)

    def page_spec(shape, p):
        return pl.BlockSpec((1, 1) + shape, lambda s, j, pt: (layer, pt[s * n_pages + j * npg + p], 0, 0))

    in_specs = [
        pl.BlockSpec((1, nheads, kvl), lambda s, j, pt: (s, 0, 0)),
        pl.BlockSpec((1, nheads, rope), lambda s, j, pt: (s, 0, 0)),
        pl.BlockSpec((nk, kvl), lambda s, j, pt: (0, 0)),
    ]
    in_specs += [page_spec((page, kvl), p) for p in range(npg)]
    in_specs += [page_spec((rope, page), p) for p in range(npg)]
    out_spec = lambda w: pl.BlockSpec((1, nheads, w), lambda s, j, pt: (s, 0, 0))
    grid_spec = pltpu.PrefetchScalarGridSpec(
        num_scalar_prefetch=1,
        grid=(db, n_pages // npg),
        in_specs=in_specs,
        out_specs=[out_spec(kvl), out_spec(LANE), out_spec(LANE)],
        scratch_shapes=[
            pltpu.VMEM((nk + nheads, kvl), BF16),
            pltpu.VMEM((npg * page, kvl), BF16),
            pltpu.VMEM((nheads, npg * page), F32),
            pltpu.VMEM((nheads, 1), F32),
            pltpu.VMEM((nheads, 1), F32),
            pltpu.VMEM((nheads, kvl), F32),
        ],
    )
    return pl.pallas_call(
        body,
        grid_spec=grid_spec,
        out_shape=[
            jax.ShapeDtypeStruct((db, nheads, kvl), F32),
            jax.ShapeDtypeStruct((db, nheads, LANE), F32),
            jax.ShapeDtypeStruct((db, nheads, LANE), F32),
        ],
        compiler_params=_cparams("arbitrary", "arbitrary"),
        name="paged_decode",
    )(pt_flat, qa, qp, wukt, *([cache_ckv] * npg), *([cache_kpe_t] * npg))


def _decode_finish_body(acc_ref, m_ref, l_ref, self_ref, v_ref, wuv_ref, o_ref, *, nheads, av):
    for h in range(nheads):
        m_old = m_ref[h][:, 0:1]
        ls = self_ref[h][:, 0:1]
        m_new = jnp.maximum(m_old, ls)
        alpha = jnp.exp(m_old - m_new)
        ps = jnp.exp(ls - m_new)
        denom = alpha * l_ref[h][:, 0:1] + ps
        ctx = (acc_ref[h] * alpha).astype(BF16)
        o_ref[:, h * av:(h + 1) * av] = (_bdot(ctx, wuv_ref[h]) + ps * v_ref[h]) / denom


def decode_finish(acc, m, l, self_logit, v_self, wuv):
    nheads, db, kvl = acc.shape
    av = wuv.shape[-1]
    full = lambda shape: pl.BlockSpec(shape, lambda i: (0,) * len(shape))
    return pl.pallas_call(
        functools.partial(_decode_finish_body, nheads=nheads, av=av),
        grid=(1,),
        in_specs=[full(acc.shape), full(m.shape), full(l.shape), full(self_logit.shape),
                  full(v_self.shape), full(wuv.shape)],
        out_specs=full((db, nheads * av)),
        out_shape=jax.ShapeDtypeStruct((db, nheads * av), F32),
        compiler_params=_cparams("arbitrary"),
        name="decode_finish",
    )(acc, m, l, self_logit, v_self, wuv)


def _rot_cols(w):
    half = w.shape[-1] // 2
    return jnp.concatenate([-w[..., half:], w[..., :half]], axis=-1)


def _pad_cols(w, n):
    return jnp.pad(w, ((0, 0), (0, n - w.shape[1])))


def _layer_weights(l, dims, norm_g, w_ffn_up, w_ffn_down, w_in, b_if, g_mh, g_cq, w_uq, g_ckv, w_ukv,
                   g_q, g_k, w_gate, b_gate, w_out, w_ple, w_pg):
    d, dff, fp = dims["d"], dims["dff"], dims["fp"]
    nh, dqk, dv = dims["nh"], dims["dqk"], dims["dv"]
    ah, ql, kvl, nope, rope, av = dims["ah"], dims["ql"], dims["kvl"], dims["nope"], dims["rope"], dims["av"]
    w = {}
    w["g"] = norm_g[l].reshape(4, 1, d)
    for i in range(2):
        up, down = w_ffn_up[l, i], w_ffn_down[l, i]
        w[f"wa{i}"] = _pad_cols(up[:, :dff], fp).astype(BF16)
        w[f"wb{i}"] = _pad_cols(up[:, dff:], fp).astype(BF16)
        w[f"wd{i}"] = jnp.pad(down, ((0, fp - dff), (0, 0))).astype(BF16)
    wi = w_in[l]
    o = [0]
    for width in (nh * dqk, nh * dqk, nh * dv, d, nh, nh, ql, kvl, rope):
        o.append(o[-1] + width)
    zq, zk, zv, zo, zi, zf, zcq, zckv, zkr = (wi[:, o[i]:o[i + 1]] for i in range(9))
    cols = [zq, zk * (dqk ** -0.5), zv, zo, zcq, zckv, zkr, _rot_cols(zkr), zi, zf]
    win = jnp.concatenate(cols, axis=1)
    w["win"] = _pad_cols(win, dims["nin"]).astype(BF16)
    w["b_col"] = b_if[l].reshape(1, 2 * nh)
    w["b_row"] = b_if[l].reshape(2 * nh, 1)
    w["gmh"] = g_mh[l].reshape(1, nh * dv)
    w["gcq"] = g_cq[l].reshape(1, ql)
    w["gckv"] = g_ckv[l].reshape(1, kvl)
    zpad = jnp.zeros((rope,), F32)
    w["gq"] = jnp.concatenate([g_q[l], zpad]).reshape(1, 2 * LANE)
    w["gk"] = jnp.concatenate([g_k[l], zpad]).reshape(1, 2 * LANE)
    uq = w_uq[l]
    pe = uq[:, :, nope:]
    w["wq"] = jnp.concatenate([uq[:, :, :nope], pe, _rot_cols(pe)], axis=-1).reshape(ql, ah * 2 * LANE).astype(BF16)
    ukv = w_ukv[l]
    w["wkv"] = ukv.reshape(kvl, ah * (nope + av)).astype(BF16)
    w["wukt"] = ukv[:, :, :nope].reshape(kvl, ah * nope).T.astype(BF16)
    w["wuv"] = jnp.transpose(ukv[:, :, nope:], (1, 0, 2)).astype(BF16)
    w["wgm"] = w_gate[l][:, :d].astype(BF16)
    w["wga"] = w_gate[l][:, d:].astype(BF16)
    w["bgm"] = b_gate[l][:d].reshape(1, d)
    w["bga"] = b_gate[l][d:].reshape(1, d)
    w["wo"] = w_out[l].astype(BF16)
    w["wple"] = w_ple[l].astype(BF16)
    w["wpg"] = w_pg[l].astype(BF16)
    return w


def _rope_table(pos, rope):
    half = rope // 2
    inv = ROPE_THETA ** (-jnp.arange(half, dtype=F32) / half)
    ang = pos.astype(F32)[:, None] * inv[None, :]
    cos, sin = jnp.cos(ang), jnp.sin(ang)
    return jnp.concatenate([cos, cos, sin, sin], axis=1)


def kernel(x_prompt, x_sample, cache_ckv, cache_kpe, state_C, state_n, state_m, page_table,
           p_prompt, p_sample, norm_g, w_ffn_up, w_ffn_down, w_in, b_if, g_mh, g_cq, w_uq,
           g_ckv, w_ukv, g_q, g_k, w_gate, b_gate, w_out, w_ple, w_pg):
    bsz, seq, d = x_prompt.shape
    db, dseq, _ = x_sample.shape
    assert dseq == 1
    depth, _, nh, dqk, dv = state_C.shape
    _, ql, ah, qk_head = w_uq.shape
    kvl = w_ukv.shape[1]
    rope = cache_kpe.shape[-1]
    nope = qk_head - rope
    av = w_ukv.shape[-1] - nope
    dff = w_ffn_down.shape[2]
    page = cache_ckv.shape[2]
    n_pages = page_table.shape[1]
    past = n_pages * page
    pd = p_prompt.shape[-1]
    col_cq = 2 * nh * dqk + nh * dv + d
    n_used = col_cq + ql + kvl + 2 * rope + 2 * nh
    col_gate = col_cq + ql + kvl + 2 * rope
    tn_in = 1536 if d >= 1536 else 256
    dims = dict(d=d, dff=dff, fp=-(-dff // 512) * 512 if dff > 512 else dff, nh=nh, dqk=dqk, dv=dv, ah=ah,
                ql=ql, kvl=kvl, nope=nope, rope=rope, av=av, nin=-(-n_used // tn_in) * tn_in)
    cl = _tile(seq, 256)
    nc = seq // cl

    cs_p = _rope_table(jnp.arange(seq), rope)
    cs_s = jnp.broadcast_to(_rope_table(past + jnp.arange(1), rope), (db, 2 * rope))
    pt_flat = page_table.reshape(-1).astype(jnp.int32)
    cache_kpe_t = jnp.swapaxes(cache_kpe, 2, 3)

    xp = x_prompt.reshape(bsz * seq, d)
    xs = x_sample.reshape(db, d)
    pp = p_prompt.reshape(depth, bsz * seq, pd)
    ps = p_sample.reshape(depth, db, pd)
    outs_p, outs_s = [], []
    c_all = None
    for l in range(depth):
        w = _layer_weights(l, dims, norm_g, w_ffn_up, w_ffn_down, w_in, b_if, g_mh, g_cq, w_uq, g_ckv,
                           w_ukv, g_q, g_k, w_gate, b_gate, w_out, w_ple, w_pg)
        mla = dict(col_cq=col_cq, nheads=ah, nope=nope, rope=rope, av=av)

        xp = ffn(xp, w["g"][0], w["wa0"], w["wb0"], w["wd0"])
        z = in_proj(xp, w["g"][1], w["win"], tn_pref=tn_in)
        z3 = z.reshape(bsz, seq, -1)
        gates = z3[:, :, col_gate:col_gate + 2 * nh]
        gates_r = jnp.swapaxes(gates.reshape(bsz, nc, cl, 2 * nh), 2, 3)
        ym, c_p, n_p, m_p = mlstm_prompt(z3, gates, gates_r, w["b_col"], w["b_row"], w["gmh"],
                                         nh=nh, dqk=dqk, dv=dv, cl=cl)
        qh, kh, vh, ckv_p, kpe_p = mla_prep(z3, cs_p, w["gcq"], w["gckv"], w["gq"], w["gk"],
                                            w["wq"], w["wkv"], **mla)
        ya = flash_attention(qh, kh, vh)
        xp = merge(xp, w["g"][1], w["wgm"], w["wga"], w["bgm"], w["bga"],
                   ym.reshape(bsz * seq, d), ya.reshape(bsz * seq, d), w["wo"])
        xp = ffn(xp, w["g"][2], w["wa1"], w["wb1"], w["wd1"], (w["g"][3], w["wpg"], pp[l], w["wple"]))
        outs_p.append((ckv_p, kpe_p, c_p, n_p, m_p[:, :, 0]))

        xs = ffn(xs, w["g"][0], w["wa0"], w["wb0"], w["wd0"])
        zs = in_proj(xs, w["g"][1], w["win"], tn_pref=tn_in)
        zs3 = zs.reshape(db, 1, -1)
        ym_s, c_all, n_s, m_s = mlstm_step(zs3, zs3[:, :, col_gate:col_gate + 2 * nh], w["b_col"], w["gmh"],
                                           state_C, state_n[l], state_m[l], l, c_all, nh=nh, dqk=dqk, dv=dv)
        qa, qp, self_logit, v_self, ckv_s, kpe_s = sample_prep(
            zs, cs_s, w["gcq"], w["gckv"], w["gq"], w["gk"], w["wq"], w["wkv"], w["wukt"], **mla)
        acc, m_run, l_run = paged_decode(pt_flat, jnp.swapaxes(qa, 0, 1), jnp.swapaxes(qp, 0, 1), w["wukt"],
                                         cache_ckv, cache_kpe_t, l, n_pages=n_pages, nheads=ah,
                                         nope=nope, rope=rope)
        ya_s = decode_finish(jnp.swapaxes(acc, 0, 1), jnp.swapaxes(m_run, 0, 1), jnp.swapaxes(l_run, 0, 1),
                             self_logit, v_self, w["wuv"])
        xs = merge(xs, w["g"][1], w["wgm"], w["wga"], w["bgm"], w["bga"], ym_s.reshape(db, d), ya_s, w["wo"])
        xs = ffn(xs, w["g"][2], w["wa1"], w["wb1"], w["wd1"], (w["g"][3], w["wpg"], ps[l], w["wple"]))
        outs_s.append((ckv_s.reshape(db, 1, kvl), kpe_s.reshape(db, 1, rope), None, n_s, m_s.reshape(db, nh)))

    stack = lambda outs, i: jnp.stack([o[i] for o in outs])
    return (xp.reshape(bsz, seq, d), xs.reshape(db, 1, d),
            stack(outs_p, 0), stack(outs_p, 1), stack(outs_p, 2), stack(outs_p, 3), stack(outs_p, 4),
            stack(outs_s, 0), stack(outs_s, 1), c_all, stack(outs_s, 3), stack(outs_s, 4))
```

```python
import functools
import math

import jax
import jax.numpy as jnp
from jax import lax
from jax.experimental import pallas as pl
from jax.experimental.pallas import tpu as pltpu

F32 = jnp.float32
BF16 = jnp.bfloat16
EPS = 1e-6
ROPE_THETA = 10000.0
LANE = 128
VMEM_LIMIT_BYTES = 56 * 2**20
_NT = (((1,), (1,)), ((), ()))


def _cparams(*sem):
    return pltpu.CompilerParams(dimension_semantics=sem, vmem_limit_bytes=VMEM_LIMIT_BYTES)


def _rms(x, g):
    return x * lax.rsqrt(jnp.mean(x * x, axis=-1, keepdims=True) + EPS) * g


def _log_sigmoid(x):
    return jnp.minimum(x, 0.0) - jnp.log(1.0 + jnp.exp(-jnp.abs(x)))


def _bdot(a, b):
    return jnp.dot(a, b, preferred_element_type=F32)


def _tile(n, pref):
    if n <= pref:
        return n
    t = pref
    while n % t:
        t -= 8
    return t


def _ffn_body(x_ref, g_ref, wa_ref, wb_ref, wd_ref, *rest, with_ple):
    o_ref, xn_ref, acc_ref = rest[-3:]
    j = pl.program_id(1)

    @pl.when(j == 0)
    def _():
        xn_ref[...] = _rms(x_ref[...], g_ref[...]).astype(BF16)
        acc_ref[...] = jnp.zeros_like(acc_ref)

    xn = xn_ref[...]
    a = _bdot(xn, wa_ref[...])
    b = _bdot(xn, wb_ref[...])
    h = (a * jax.nn.sigmoid(a) * b).astype(BF16)
    acc_ref[...] += _bdot(h, wd_ref[...])

    @pl.when(j == pl.num_programs(1) - 1)
    def _():
        y = x_ref[...] + 0.5 * acc_ref[...]
        if with_ple:
            g2_ref, wpg_ref, p_ref, wple_ref = rest[:4]
            gate = jax.nn.sigmoid(_bdot(_rms(y, g2_ref[...]).astype(BF16), wpg_ref[...]))
            y = y + gate * _bdot(p_ref[...].astype(BF16), wple_ref[...])
        o_ref[...] = y


def _norm_spec(d, layer, k):
    return pl.BlockSpec((None, None, 1, d), lambda i, j: (layer, k, 0, 0))


def ffn(x, gains, w_up, w_down, layer, idx, ple=None, *, tm_pref=512, tf_pref=512):
    m, d = x.shape
    fp = w_down.shape[2]
    tm, tf = _tile(m, tm_pref), _tile(fp, tf_pref)
    nf = fp // tf
    in_specs = [
        pl.BlockSpec((tm, d), lambda i, j: (i, 0)),
        _norm_spec(d, layer, 2 * idx),
        pl.BlockSpec((None, None, d, tf), lambda i, j: (layer, idx, 0, j)),
        pl.BlockSpec((None, None, d, tf), lambda i, j: (layer, idx, 0, nf + j)),
        pl.BlockSpec((None, None, tf, d), lambda i, j: (layer, idx, j, 0)),
    ]
    args = [x, gains, w_up, w_up, w_down]
    if ple is not None:
        wpg, p, wple = ple
        pd = p.shape[-1]
        in_specs += [
            _norm_spec(d, layer, 3),
            pl.BlockSpec((None, d, d), lambda i, j: (layer, 0, 0), pipeline_mode=pl.Buffered(1)),
            pl.BlockSpec((None, tm, pd), lambda i, j: (layer, i, 0)),
            pl.BlockSpec((None, pd, d), lambda i, j: (layer, 0, 0), pipeline_mode=pl.Buffered(1)),
        ]
        args += [gains, wpg, p, wple]
    return pl.pallas_call(
        functools.partial(_ffn_body, with_ple=ple is not None),
        grid=(m // tm, fp // tf),
        in_specs=in_specs,
        out_specs=pl.BlockSpec((tm, d), lambda i, j: (i, 0)),
        out_shape=jax.ShapeDtypeStruct((m, d), F32),
        scratch_shapes=[pltpu.VMEM((tm, d), BF16), pltpu.VMEM((tm, d), F32)],
        compiler_params=_cparams("parallel", "arbitrary"),
        name="ffn_ple" if ple is not None else "ffn",
    )(*args)


def _in_proj_body(x_ref, g_ref, w_ref, o_ref, xn_ref):
    @pl.when(pl.program_id(1) == 0)
    def _():
        xn_ref[...] = _rms(x_ref[...], g_ref[...]).astype(BF16)

    o_ref[...] = _bdot(xn_ref[...], w_ref[...])


def in_proj(x, gains, w, layer, *, tm_pref=512, tn_pref=1536):
    m, d = x.shape
    n = w.shape[-1]
    tm, tn = _tile(m, tm_pref), _tile(n, tn_pref)
    return pl.pallas_call(
        _in_proj_body,
        grid=(m // tm, n // tn),
        in_specs=[
            pl.BlockSpec((tm, d), lambda i, j: (i, 0)),
            _norm_spec(d, layer, 1),
            pl.BlockSpec((None, d, tn), lambda i, j: (layer, 0, j)),
        ],
        out_specs=pl.BlockSpec((tm, tn), lambda i, j: (i, j)),
        out_shape=jax.ShapeDtypeStruct((m, n), F32),
        scratch_shapes=[pltpu.VMEM((tm, d), BF16)],
        compiler_params=_cparams("parallel", "arbitrary"),
        name="in_proj",
    )(x, gains, w)


def _merge_body(x_ref, g_ref, wgm_ref, wga_ref, bm_ref, ba_ref, ym_ref, ya_ref, wo_ref,
                o_ref, xn_ref, mix_ref, *, tn):
    j = pl.program_id(1)

    @pl.when(j == 0)
    def _():
        xn_ref[...] = _rms(x_ref[...], g_ref[...]).astype(BF16)

    xn = xn_ref[...]
    gm = jax.nn.sigmoid(_bdot(xn, wgm_ref[...]) + bm_ref[...])
    ga = jax.nn.sigmoid(_bdot(xn, wga_ref[...]) + ba_ref[...])
    col = pl.multiple_of(j * tn, LANE)
    mix_ref[:, pl.ds(col, tn)] = (gm * ym_ref[...] + ga * ya_ref[...]).astype(BF16)

    @pl.when(j == pl.num_programs(1) - 1)
    def _():
        o_ref[...] = x_ref[...] + _bdot(mix_ref[...], wo_ref[...])


def merge(x, gains, w_gate, b_gate, ym, ya, w_out, layer, *, tm_pref=512, tn_pref=512):
    m, d = x.shape
    tm, tn = _tile(m, tm_pref), _tile(d, tn_pref)
    nj = d // tn
    return pl.pallas_call(
        functools.partial(_merge_body, tn=tn),
        grid=(m // tm, nj),
        in_specs=[
            pl.BlockSpec((tm, d), lambda i, j: (i, 0)),
            _norm_spec(d, layer, 1),
            pl.BlockSpec((None, d, tn), lambda i, j: (layer, 0, j)),
            pl.BlockSpec((None, d, tn), lambda i, j: (layer, 0, nj + j)),
            pl.BlockSpec((None, 1, tn), lambda i, j: (layer, 0, j)),
            pl.BlockSpec((None, 1, tn), lambda i, j: (layer, 0, nj + j)),
            pl.BlockSpec((tm, tn), lambda i, j: (i, j)),
            pl.BlockSpec((tm, tn), lambda i, j: (i, j)),
            pl.BlockSpec((None, d, d), lambda i, j: (layer, 0, 0), pipeline_mode=pl.Buffered(1)),
        ],
        out_specs=pl.BlockSpec((tm, d), lambda i, j: (i, 0)),
        out_shape=jax.ShapeDtypeStruct((m, d), F32),
        scratch_shapes=[pltpu.VMEM((tm, d), BF16), pltpu.VMEM((tm, d), BF16)],
        compiler_params=_cparams("parallel", "arbitrary"),
        name="merge",
    )(x, gains, w_gate, w_gate, b_gate, b_gate, ym, ya, w_out)


def _mlstm_chunk_body(q_ref, k_ref, v_ref, zo_ref, gc_ref, gr_ref, bc_ref, br_ref, gmh_ref,
                      ym_ref, c_out, n_out, m_out, c_scr, n_scr, m_scr, *, nh, dqk, dv, cl):
    c = pl.program_id(1)

    @pl.when(c == 0)
    def _():
        c_scr[...] = jnp.zeros_like(c_scr)
        n_scr[...] = jnp.zeros_like(n_scr)
        m_scr[...] = jnp.zeros_like(m_scr)

    gc = gc_ref[0] + bc_ref[...]
    gr = gr_ref[0, 0] + br_ref[...]
    t_idx = lax.broadcasted_iota(jnp.int32, (cl, cl), 0)
    j_idx = lax.broadcasted_iota(jnp.int32, (cl, cl), 1)
    causal = j_idx <= t_idx
    for h in range(nh):
        i_col = gc[:, h:h + 1]
        f_col = _log_sigmoid(gc[:, nh + h:nh + h + 1])
        i_row = gr[h:h + 1, :]
        f_row = _log_sigmoid(gr[nh + h:nh + h + 1, :])
        b_col = jnp.sum(jnp.where(causal, f_row, 0.0), axis=1, keepdims=True)
        b_row = jnp.sum(jnp.where(t_idx <= j_idx, f_col, 0.0), axis=0, keepdims=True)
        m_prev = m_scr[h:h + 1, 0:1]
        d = jnp.where(causal, b_col - b_row + i_row, -jnp.inf)
        inter = b_col + m_prev
        m_t = jnp.maximum(inter, jnp.max(d, axis=1, keepdims=True))
        w_inter = jnp.exp(inter - m_t)
        qh = q_ref[0, :, h * dqk:(h + 1) * dqk]
        kh = k_ref[0, :, h * dqk:(h + 1) * dqk]
        vb = v_ref[0, :, h * dv:(h + 1) * dv].astype(BF16)
        qb = qh.astype(BF16)
        s = lax.dot_general(qb, kh.astype(BF16), _NT, preferred_element_type=F32) * jnp.exp(d - m_t)
        c_old = c_scr[h]
        n_old = n_scr[h:h + 1, :]
        num = w_inter * _bdot(qb, c_old.astype(BF16)) + _bdot(s.astype(BF16), vb)
        den = w_inter * jnp.sum(qh * n_old, axis=1, keepdims=True) + jnp.sum(s, axis=1, keepdims=True)
        hh = num / jnp.maximum(jnp.abs(den), jnp.exp(-m_t))
        hn = _rms(hh, gmh_ref[:, h * dv:(h + 1) * dv])
        ym_ref[0, :, h * dv:(h + 1) * dv] = jax.nn.sigmoid(zo_ref[0, :, h * dv:(h + 1) * dv]) * hn
        b_last = b_col[cl - 1:cl, :]
        g_col = b_last - b_col + i_col
        g_row = b_last - b_row + i_row
        m_new = jnp.maximum(b_last + m_prev, jnp.max(g_row, axis=1, keepdims=True))
        a_prev = jnp.exp(b_last + m_prev - m_new)
        ka = kh * jnp.exp(g_col - m_new)
        c_scr[h] = a_prev * c_old + _bdot(ka.T.astype(BF16), vb)
        n_scr[h:h + 1, :] = a_prev * n_old + jnp.sum(ka, axis=0, keepdims=True)
        m_scr[h:h + 1, :] = jnp.broadcast_to(m_new, (1, LANE))

    @pl.when(c == pl.num_programs(1) - 1)
    def _():
        c_out[0] = c_scr[...]
        n_out[0] = n_scr[...]
        m_out[0] = m_scr[...]


def mlstm_prompt(z3, gc, gr, bc, br, gmh, *, nh, dqk, dv, cl):
    bsz, s, _ = z3.shape
    nc = s // cl
    wq, wv = nh * dqk, nh * dv
    assert wq % LANE == 0 and wv % wq == 0
    kq, kv = wq // wq, wv // wq
    v_blk = (2 * wq) // wv
    assert (2 * wq) % wv == 0
    body = functools.partial(_mlstm_chunk_body, nh=nh, dqk=dqk, dv=dv, cl=cl)
    return pl.pallas_call(
        body,
        grid=(bsz, nc),
        in_specs=[
            pl.BlockSpec((1, cl, wq), lambda b, c: (b, c, 0)),
            pl.BlockSpec((1, cl, wq), lambda b, c: (b, c, kq)),
            pl.BlockSpec((1, cl, wv), lambda b, c: (b, c, v_blk)),
            pl.BlockSpec((1, cl, wv), lambda b, c: (b, c, v_blk + 1)),
            pl.BlockSpec((1, cl, 2 * nh), lambda b, c: (b, c, 0)),
            pl.BlockSpec((1, 1, 2 * nh, cl), lambda b, c: (b, c, 0, 0)),
            pl.BlockSpec((1, 2 * nh), lambda b, c: (0, 0)),
            pl.BlockSpec((2 * nh, 1), lambda b, c: (0, 0)),
            pl.BlockSpec((1, wv), lambda b, c: (0, 0)),
        ],
        out_specs=[
            pl.BlockSpec((1, cl, wv), lambda b, c: (b, c, 0)),
            pl.BlockSpec((1, nh, dqk, dv), lambda b, c: (b, 0, 0, 0)),
            pl.BlockSpec((1, nh, dqk), lambda b, c: (b, 0, 0)),
            pl.BlockSpec((1, nh, LANE), lambda b, c: (b, 0, 0)),
        ],
        out_shape=[
            jax.ShapeDtypeStruct((bsz, s, wv), F32),
            jax.ShapeDtypeStruct((bsz, nh, dqk, dv), F32),
            jax.ShapeDtypeStruct((bsz, nh, dqk), F32),
            jax.ShapeDtypeStruct((bsz, nh, LANE), F32),
        ],
        scratch_shapes=[pltpu.VMEM((nh, dqk, dv), F32), pltpu.VMEM((nh, dqk), F32),
                        pltpu.VMEM((nh, LANE), F32)],
        compiler_params=_cparams("parallel", "arbitrary"),
        name="mlstm_prompt",
    )(z3, z3, z3, z3, gc, gr, bc, br, gmh)


def _mlstm_step_body(q_ref, k_ref, v_ref, zo_ref, gs_ref, bias_ref, gmh_ref, c_ref, n_ref, m_ref, *rest,
                     nh, dqk, dv):
    ym_ref, c_out, n_out, m_out = rest[-4:]
    c_ref, c_out = c_ref.at[0], c_out.at[0]
    gates = gs_ref[0] + bias_ref[...]
    eye = (lax.broadcasted_iota(jnp.int32, (dqk, dqk), 0)
           == lax.broadcasted_iota(jnp.int32, (dqk, dqk), 1))

    def as_column(row):
        return jnp.sum(jnp.where(eye, row, 0.0), axis=1, keepdims=True)

    for h in range(nh):
        i_pre = gates[:, h:h + 1]
        logf = _log_sigmoid(gates[:, nh + h:nh + h + 1])
        m_prev = m_ref[0, :, h:h + 1]
        inter = logf + m_prev
        m_t = jnp.maximum(inter, i_pre)
        a_prev = jnp.exp(inter - m_t)
        a_new = jnp.exp(i_pre - m_t)
        qh = q_ref[0, :, h * dqk:(h + 1) * dqk]
        kh = k_ref[0, :, h * dqk:(h + 1) * dqk] * a_new
        vh = v_ref[0, :, h * dv:(h + 1) * dv]
        c_new = a_prev * c_ref[0, h] + as_column(kh) * vh
        n_new = a_prev * n_ref[0, h:h + 1, :] + kh
        num = jnp.sum(as_column(qh) * c_new, axis=0, keepdims=True)
        den = jnp.sum(qh * n_new, axis=1, keepdims=True)
        hh = num / jnp.maximum(jnp.abs(den), jnp.exp(-m_t))
        hn = _rms(hh, gmh_ref[:, h * dv:(h + 1) * dv])
        ym_ref[0, :, h * dv:(h + 1) * dv] = jax.nn.sigmoid(zo_ref[0, :, h * dv:(h + 1) * dv]) * hn
        c_out[0, h] = c_new
        n_out[0, h:h + 1, :] = n_new
        m_out[0, :, h:h + 1] = m_t


def mlstm_step(z3, gs, bias, gmh, state_c, n0, m0, layer, c_all=None, *, nh, dqk, dv):
    db = z3.shape[0]
    wq, wv = nh * dqk, nh * dv
    v_blk = (2 * wq) // wv
    body = functools.partial(_mlstm_step_body, nh=nh, dqk=dqk, dv=dv)
    m3 = m0.reshape(db, 1, nh)
    c_spec = pl.BlockSpec((1, 1, nh, dqk, dv), lambda s: (layer, s, 0, 0, 0))
    in_specs = [
        pl.BlockSpec((1, 1, wq), lambda s: (s, 0, 0)),
        pl.BlockSpec((1, 1, wq), lambda s: (s, 0, 1)),
        pl.BlockSpec((1, 1, wv), lambda s: (s, 0, v_blk)),
        pl.BlockSpec((1, 1, wv), lambda s: (s, 0, v_blk + 1)),
        pl.BlockSpec((1, 1, 2 * nh), lambda s: (s, 0, 0)),
        pl.BlockSpec((1, 2 * nh), lambda s: (0, 0)),
        pl.BlockSpec((1, wv), lambda s: (0, 0)),
        c_spec,
        pl.BlockSpec((1, nh, dqk), lambda s: (s, 0, 0)),
        pl.BlockSpec((1, 1, nh), lambda s: (s, 0, 0)),
    ]
    args = [z3, z3, z3, z3, gs, bias, gmh, state_c, n0, m3]
    aliases = {}
    if c_all is not None:
        in_specs.append(pl.BlockSpec(memory_space=pl.ANY))
        args.append(c_all)
        aliases = {len(args) - 1: 1}
    return pl.pallas_call(
        body,
        grid=(db,),
        in_specs=in_specs,
        out_specs=[
            pl.BlockSpec((1, 1, wv), lambda s: (s, 0, 0)),
            c_spec,
            pl.BlockSpec((1, nh, dqk), lambda s: (s, 0, 0)),
            pl.BlockSpec((1, 1, nh), lambda s: (s, 0, 0)),
        ],
        out_shape=[
            jax.ShapeDtypeStruct((db, 1, wv), F32),
            jax.ShapeDtypeStruct(state_c.shape, F32),
            jax.ShapeDtypeStruct((db, nh, dqk), F32),
            jax.ShapeDtypeStruct((db, 1, nh), F32),
        ],
        input_output_aliases=aliases,
        compiler_params=_cparams("parallel"),
        name="mlstm_step",
    )(*args)


def _rope_half(xb, cs):
    t = xb * cs
    return t + pltpu.roll(t, 64, axis=1)


def _query_heads(cq_b, wq_ref, cs, gq_ref, h, nope, qk_head, low):
    w = 2 * LANE
    qf = _bdot(cq_b, wq_ref[:, h * w:(h + 1) * w])
    qn = qf[:, :nope]
    qp = jnp.where(low, _rope_half(qf[:, nope:], cs), 0.0)
    ssq = jnp.sum(qn * qn, axis=1, keepdims=True) + jnp.sum(qp * qp, axis=1, keepdims=True)
    r = lax.rsqrt(ssq / qk_head + EPS) * (qk_head ** -0.5)
    return qn * r * gq_ref[:, :nope], qp * r * gq_ref[:, nope:]


def _key_heads(ckv_b, wkv_ref, kpe, kpe_sq, gk_ref, h, nope, qk_head):
    w = 2 * LANE
    kvf = _bdot(ckv_b, wkv_ref[:, h * w:(h + 1) * w])
    kn = kvf[:, :nope]
    r = lax.rsqrt((jnp.sum(kn * kn, axis=1, keepdims=True) + kpe_sq) / qk_head + EPS)
    return kn * r * gk_ref[:, :nope], kpe * r * gk_ref[:, nope:], kvf[:, nope:]


def _mla_prep_body(zcq_ref, zckv_ref, zkr_ref, cs_ref, gcq_ref, gckv_ref, gq_ref, gk_ref,
                   wq_ref, wkv_ref, q_out, k_out, v_out, ckv_out, kpe_out, *, nheads, nope, rope):
    qk_head = nope + rope
    cs = cs_ref[...]
    tm = cs.shape[0]
    low = lax.broadcasted_iota(jnp.int32, (tm, LANE), 1) < rope
    cq_b = _rms(zcq_ref[0], gcq_ref[...]).astype(BF16)
    ckv = _rms(zckv_ref[0], gckv_ref[...])
    ckv_out[0] = ckv
    ckv_b = ckv.astype(BF16)
    kpe = jnp.where(low, _rope_half(zkr_ref[0], cs), 0.0)
    kpe_out[0] = kpe[:, :rope]
    kpe_sq = jnp.sum(kpe * kpe, axis=1, keepdims=True)
    for h in range(nheads):
        qn, qp = _query_heads(cq_b, wq_ref, cs, gq_ref, h, nope, qk_head, low)
        q_out[0, h, :, :nope] = qn.astype(BF16)
        q_out[0, h, :, nope:] = qp.astype(BF16)
        kn, kp, v = _key_heads(ckv_b, wkv_ref, kpe, kpe_sq, gk_ref, h, nope, qk_head)
        k_out[0, h, :, :nope] = kn.astype(BF16)
        k_out[0, h, :, nope:] = kp.astype(BF16)
        v_out[0, h, :, :nope] = v.astype(BF16)
        v_out[0, h, :, nope:] = jnp.ones_like(v, BF16)


def mla_prep(z3, cs, gcq, gckv, gq, gk, wq, wkv, *, col_cq, nheads, nope, rope, av, tm_pref=256):
    bsz, s, _ = z3.shape
    ql, kvl = wq.shape[0], wkv.shape[0]
    assert nope == LANE and 2 * rope == LANE and av == LANE and ql == kvl
    tm = _tile(s, tm_pref)
    cb = col_cq // ql
    kr_blk = (col_cq + ql + kvl) // LANE
    body = functools.partial(_mla_prep_body, nheads=nheads, nope=nope, rope=rope)
    hw = 2 * LANE
    return pl.pallas_call(
        body,
        grid=(bsz, s // tm),
        in_specs=[
            pl.BlockSpec((1, tm, ql), lambda b, i: (b, i, cb)),
            pl.BlockSpec((1, tm, kvl), lambda b, i: (b, i, cb + 1)),
            pl.BlockSpec((1, tm, LANE), lambda b, i: (b, i, kr_blk)),
            pl.BlockSpec((tm, LANE), lambda b, i: (i, 0)),
            pl.BlockSpec((1, ql), lambda b, i: (0, 0)),
            pl.BlockSpec((1, kvl), lambda b, i: (0, 0)),
            pl.BlockSpec((1, hw), lambda b, i: (0, 0)),
            pl.BlockSpec((1, hw), lambda b, i: (0, 0)),
            pl.BlockSpec((ql, nheads * hw), lambda b, i: (0, 0)),
            pl.BlockSpec((kvl, nheads * hw), lambda b, i: (0, 0)),
        ],
        out_specs=[
            pl.BlockSpec((1, nheads, tm, hw), lambda b, i: (b, 0, i, 0)),
            pl.BlockSpec((1, nheads, tm, hw), lambda b, i: (b, 0, i, 0)),
            pl.BlockSpec((1, nheads, tm, 2 * av), lambda b, i: (b, 0, i, 0)),
            pl.BlockSpec((1, tm, kvl), lambda b, i: (b, i, 0)),
            pl.BlockSpec((1, tm, rope), lambda b, i: (b, i, 0)),
        ],
        out_shape=[
            jax.ShapeDtypeStruct((bsz, nheads, s, hw), BF16),
            jax.ShapeDtypeStruct((bsz, nheads, s, hw), BF16),
            jax.ShapeDtypeStruct((bsz, nheads, s, 2 * av), BF16),
            jax.ShapeDtypeStruct((bsz, s, kvl), F32),
            jax.ShapeDtypeStruct((bsz, s, rope), F32),
        ],
        compiler_params=_cparams("parallel", "parallel"),
        name="mla_prep",
    )(z3, z3, z3, cs, gcq, gckv, gq, gk, wq, wkv)


def _flash_body(q_ref, k_ref, v_ref, o_ref, s_scr, m_scr, acc_scr, *, tq, tk, av):
    qi = pl.program_id(2)
    m_scr[...] = jnp.full_like(m_scr, -jnp.inf)
    acc_scr[...] = jnp.zeros_like(acc_scr)
    q = q_ref[0, 0]

    def scores(ki, slot, r0=0):
        off = pl.multiple_of(ki * tk, tk)
        s_scr[slot, r0:, :] = lax.dot_general(q[r0:], k_ref[0, 0, pl.ds(off, tk), :], _NT,
                                              preferred_element_type=F32)

    def update(ki, slot, r0=0, r1=tq, triangular=False):
        off = pl.multiple_of(ki * tk, tk)
        s = s_scr[slot, r0:r1, :]
        if triangular:
            row = lax.broadcasted_iota(jnp.int32, (tk, tk), 0)
            col = lax.broadcasted_iota(jnp.int32, (tk, tk), 1)
            s = jnp.where(col <= row, s, -jnp.inf)
        m_old = m_scr[r0:r1]
        m_new = jnp.maximum(m_old, jnp.max(s, axis=1, keepdims=True))
        alpha = jnp.exp(m_old - m_new)
        p = jnp.exp(s - m_new)
        acc_scr[r0:r1] = alpha * acc_scr[r0:r1] + _bdot(p.astype(BF16), v_ref[0, 0, pl.ds(off, tk), :])
        m_scr[r0:r1] = m_new

    scores(0, 0)

    def pair(i, carry):
        scores(2 * i + 1, 1)
        update(2 * i, 0)
        scores(2 * i + 2, 0)
        update(2 * i + 1, 1)
        return carry

    lax.fori_loop(0, qi, pair, 0)
    scores(2 * qi + 1, 1, tk)
    update(2 * qi, 0, 0, tk, triangular=True)
    update(2 * qi, 0, tk, tq)
    update(2 * qi + 1, 1, tk, tq, triangular=True)
    o_ref[0] = acc_scr[:, :av] / acc_scr[:, av:]


def flash_attention(q, k, v, *, tq_pref=1024):
    bsz, nheads, s, hw = q.shape
    av = v.shape[-1] // 2
    tq = _tile(s, tq_pref)
    tk = tq // 2
    body = functools.partial(_flash_body, tq=tq, tk=tk, av=av)
    return pl.pallas_call(
        body,
        grid=(bsz, nheads, s // tq),
        in_specs=[
            pl.BlockSpec((1, 1, tq, hw), lambda b, h, i: (b, h, i, 0)),
            pl.BlockSpec((1, 1, s, hw), lambda b, h, i: (b, h, 0, 0)),
            pl.BlockSpec((1, 1, s, 2 * av), lambda b, h, i: (b, h, 0, 0)),
        ],
        out_specs=pl.BlockSpec((1, tq, av), lambda b, h, i: (b, i, h)),
        out_shape=jax.ShapeDtypeStruct((bsz, s, nheads * av), F32),
        scratch_shapes=[pltpu.VMEM((2, tq, tk), F32), pltpu.VMEM((tq, 1), F32),
                        pltpu.VMEM((tq, 2 * av), F32)],
        compiler_params=_cparams("parallel", "parallel", "arbitrary"),
        name="flash_attention",
    )(q, k, v)


def _sample_prep_body(zcq_ref, zckv_ref, zkr_ref, cs_ref, gcq_ref, gckv_ref, gq_ref, gk_ref,
                      wq_ref, wkv_ref, wukt_ref, qa_out, qp_out, self_out, v_out, ckv_out, kpe_out,
                      *, nheads, nope, rope):
    qk_head = nope + rope
    cs = cs_ref[...]
    tm = cs.shape[0]
    low = lax.broadcasted_iota(jnp.int32, (tm, LANE), 1) < rope
    cq_b = _rms(zcq_ref[...], gcq_ref[...]).astype(BF16)
    ckv = _rms(zckv_ref[...], gckv_ref[...])
    ckv_out[...] = ckv
    ckv_b = ckv.astype(BF16)
    kpe = jnp.where(low, _rope_half(zkr_ref[...], cs), 0.0)
    kpe_out[...] = kpe[:, :rope]
    kpe_sq = jnp.sum(kpe * kpe, axis=1, keepdims=True)
    for h in range(nheads):
        qn, qp = _query_heads(cq_b, wq_ref, cs, gq_ref, h, nope, qk_head, low)
        kn, kp, v = _key_heads(ckv_b, wkv_ref, kpe, kpe_sq, gk_ref, h, nope, qk_head)
        self_out[h] = jnp.broadcast_to(
            jnp.sum(qn * kn, axis=1, keepdims=True) + jnp.sum(qp * kp, axis=1, keepdims=True), (tm, LANE))
        v_out[h] = v
        qg = (qn * gk_ref[:, :nope]).astype(BF16)
        qa_out[h] = _bdot(qg, wukt_ref[h * nope:(h + 1) * nope, :]).astype(BF16)
        qp_out[h] = (qp * gk_ref[:, nope:])[:, :rope].astype(BF16)


def sample_prep(z, cs, gcq, gckv, gq, gk, wq, wkv, wukt, *, col_cq, nheads, nope, rope, av):
    db = z.shape[0]
    ql, kvl = wq.shape[0], wkv.shape[0]
    cb = col_cq // ql
    kr_blk = (col_cq + ql + kvl) // LANE
    hw = 2 * LANE
    body = functools.partial(_sample_prep_body, nheads=nheads, nope=nope, rope=rope)
    full = lambda shape: pl.BlockSpec(shape, lambda i: (0,) * len(shape))
    return pl.pallas_call(
        body,
        grid=(1,),
        in_specs=[
            pl.BlockSpec((db, ql), lambda i: (0, cb)),
            pl.BlockSpec((db, kvl), lambda i: (0, cb + 1)),
            pl.BlockSpec((db, LANE), lambda i: (0, kr_blk)),
            full((db, LANE)), full((1, ql)), full((1, kvl)), full((1, hw)), full((1, hw)),
            full((ql, nheads * hw)), full((kvl, nheads * hw)), full((nheads * nope, kvl)),
        ],
        out_specs=[
            full((nheads, db, kvl)), full((nheads, db, rope)), full((nheads, db, LANE)),
            full((nheads, db, av)), full((db, kvl)), full((db, rope)),
        ],
        out_shape=[
            jax.ShapeDtypeStruct((nheads, db, kvl), BF16),
            jax.ShapeDtypeStruct((nheads, db, rope), BF16),
            jax.ShapeDtypeStruct((nheads, db, LANE), F32),
            jax.ShapeDtypeStruct((nheads, db, av), F32),
            jax.ShapeDtypeStruct((db, kvl), F32),
            jax.ShapeDtypeStruct((db, rope), F32),
        ],
        compiler_params=_cparams("arbitrary"),
        name="sample_prep",
    )(z, z, z, cs, gcq, gckv, gq, gk, wq, wkv, wukt)


def _decode_body(pt_ref, qa_ref, qp_ref, wukt_ref, *refs, npg, nheads, nope, rope, page, sub):
    ckv_refs = refs[:npg]
    kpe_refs = refs[npg:2 * npg]
    acc_out, m_out, l_out = refs[2 * npg:2 * npg + 3]
    lhs_scr, cb_scr, lg_scr, m_scr, l_scr, acc_scr = refs[2 * npg + 3:]
    del pt_ref
    j = pl.program_id(1)
    qk_head = nope + rope
    nk = nheads * nope
    ppt = sub // page

    @pl.when(j == 0)
    def _():
        lhs_scr[0:nk, :] = wukt_ref[...]
        lhs_scr[nk:nk + nheads, :] = qa_ref[0]
        m_scr[...] = jnp.full_like(m_scr, -jnp.inf)
        l_scr[...] = jnp.zeros_like(l_scr)
        acc_scr[...] = jnp.zeros_like(acc_scr)

    qp = qp_ref[0]
    for t in range(npg // ppt):
        pages = range(t * ppt, (t + 1) * ppt)
        cb = jnp.concatenate([ckv_refs[p][0, 0].astype(BF16) for p in pages], axis=0)
        cb_scr[t * sub:(t + 1) * sub, :] = cb
        kpt = jnp.concatenate([kpe_refs[p][0, 0] for p in pages], axis=1)
        r = lax.dot_general(lhs_scr[...], cb, _NT, preferred_element_type=F32)
        kn = r[0:nk, :].reshape(nheads, nope, sub)
        ssq = jnp.sum(kn * kn, axis=1)
        pe_sq = jnp.sum(kpt * kpt, axis=0, keepdims=True)
        s_pe = _bdot(qp, kpt.astype(BF16))
        lg_scr[:, t * sub:(t + 1) * sub] = (r[nk:nk + nheads, :] + s_pe) * lax.rsqrt(
            (ssq + pe_sq) / qk_head + EPS)

    logits = lg_scr[...]
    m_old = m_scr[...]
    m_new = jnp.maximum(m_old, jnp.max(logits, axis=1, keepdims=True))
    alpha = jnp.exp(m_old - m_new)
    pr = jnp.exp(logits - m_new)
    l_scr[...] = alpha * l_scr[...] + jnp.sum(pr, axis=1, keepdims=True)
    acc_scr[...] = alpha * acc_scr[...] + _bdot(pr.astype(BF16), cb_scr[...])
    m_scr[...] = m_new

    @pl.when(j == pl.num_programs(1) - 1)
    def _():
        acc_out[0] = acc_scr[...]
        m_out[0] = jnp.broadcast_to(m_scr[...], (nheads, LANE))
        l_out[0] = jnp.broadcast_to(l_scr[...], (nheads, LANE))


def paged_decode(pt_flat, qa, qp, wukt, cache_ckv, cache_kpe_t, layer, *, n_pages, nheads, nope, rope,
                 npg=32, sub=512):
    db = qa.shape[0]
    page, kvl = cache_ckv.shape[2], cache_ckv.shape[3]
    npg = math.gcd(npg, n_pages)
    sub = math.gcd(sub, npg * page)
    nk = nheads * nope
    body = functools.partial(_decode_body, npg=npg, nheads=nheads, nope=nope, rope=rope, page=page, sub=sub)

    def page_spec(shape, p):
        return pl.BlockSpec((1, 1) + shape, lambda s, j, pt: (layer, pt[s * n_pages + j * npg + p], 0, 0))

    in_specs = [
        pl.BlockSpec((1, nheads, kvl), lambda s, j, pt: (s, 0, 0)),
        pl.BlockSpec((1, nheads, rope), lambda s, j, pt: (s, 0, 0)),
        pl.BlockSpec((nk, kvl), lambda s, j, pt: (0, 0)),
    ]
    in_specs += [page_spec((page, kvl), p) for p in range(npg)]
    in_specs += [page_spec((rope, page), p) for p in range(npg)]
    out_spec = lambda w: pl.BlockSpec((1, nheads, w), lambda s, j, pt: (s, 0, 0))
    grid_spec = pltpu.PrefetchScalarGridSpec(
        num_scalar_prefetch=1,
        grid=(db, n_pages // npg),
        in_specs=in_specs,
        out_specs=[out_spec(kvl), out_spec(LANE), out_spec(LANE)],
        scratch_shapes=[
            pltpu.VMEM((nk + nheads, kvl), BF16),
            pltpu.VMEM((npg * page, kvl), BF16),
            pltpu.VMEM((nheads, npg * page), F32),
            pltpu.VMEM((nheads, 1), F32),
            pltpu.VMEM((nheads, 1), F32),
            pltpu.VMEM((nheads, kvl), F32),
        ],
    )
    return pl.pallas_call(
        body,
        grid_spec=grid_spec,
        out_shape=[
            jax.ShapeDtypeStruct((db, nheads, kvl), F32),
            jax.ShapeDtypeStruct((db, nheads, LANE), F32),
            jax.ShapeDtypeStruct((db, nheads, LANE), F32),
        ],
        compiler_params=_cparams("arbitrary", "arbitrary"),
        name="paged_decode",
    )(pt_flat, qa, qp, wukt, *([cache_ckv] * npg), *([cache_kpe_t] * npg))


def _decode_finish_body(acc_ref, m_ref, l_ref, self_ref, v_ref, wuv_ref, o_ref, *, nheads, av):
    for h in range(nheads):
        m_old = m_ref[h][:, 0:1]
        ls = self_ref[h][:, 0:1]
        m_new = jnp.maximum(m_old, ls)
        alpha = jnp.exp(m_old - m_new)
        ps = jnp.exp(ls - m_new)
        denom = alpha * l_ref[h][:, 0:1] + ps
        ctx = (acc_ref[h] * alpha).astype(BF16)
        o_ref[:, h * av:(h + 1) * av] = (_bdot(ctx, wuv_ref[h]) + ps * v_ref[h]) / denom


def decode_finish(acc, m, l, self_logit, v_self, wuv):
    nheads, db, kvl = acc.shape
    av = wuv.shape[-1]
    full = lambda shape: pl.BlockSpec(shape, lambda i: (0,) * len(shape))
    return pl.pallas_call(
        functools.partial(_decode_finish_body, nheads=nheads, av=av),
        grid=(1,),
        in_specs=[full(acc.shape), full(m.shape), full(l.shape), full(self_logit.shape),
                  full(v_self.shape), full(wuv.shape)],
        out_specs=full((db, nheads * av)),
        out_shape=jax.ShapeDtypeStruct((db, nheads * av), F32),
        compiler_params=_cparams("arbitrary"),
        name="decode_finish",
    )(acc, m, l, self_logit, v_self, wuv)


def _rot_cols(w):
    half = w.shape[-1] // 2
    return jnp.concatenate([-w[..., half:], w[..., :half]], axis=-1)


def _pad_last(w, n):
    return jnp.pad(w, [(0, 0)] * (w.ndim - 1) + [(0, n - w.shape[-1])])


def _stacked_weights(dims, norm_g, w_ffn_up, w_ffn_down, w_in, w_gate, b_gate, w_out, w_ple, w_pg):
    d, dff, fp, nh, dqk, dv = dims["d"], dims["dff"], dims["fp"], dims["nh"], dims["dqk"], dims["dv"]
    ql, kvl, rope = dims["ql"], dims["kvl"], dims["rope"]
    depth = norm_g.shape[0]
    s = {"gains": norm_g.reshape(depth, 4, 1, d)}
    s["w_up"] = jnp.concatenate([_pad_last(w_ffn_up[..., :dff], fp), _pad_last(w_ffn_up[..., dff:], fp)],
                                axis=-1).astype(BF16)
    s["w_down"] = jnp.pad(w_ffn_down, ((0, 0), (0, 0), (0, fp - dff), (0, 0))).astype(BF16)
    o = [0]
    for width in (nh * dqk, nh * dqk, nh * dv, d, nh, nh, ql, kvl, rope):
        o.append(o[-1] + width)
    zq, zk, zv, zo, zi, zf, zcq, zckv, zkr = (w_in[:, :, o[i]:o[i + 1]] for i in range(9))
    cols = [zq, zk * (dqk ** -0.5), zv, zo, zcq, zckv, zkr, _rot_cols(zkr), zi, zf]
    s["w_in"] = _pad_last(jnp.concatenate(cols, axis=2), dims["nin"]).astype(BF16)
    s["w_gate"] = w_gate.astype(BF16)
    s["b_gate"] = b_gate.reshape(depth, 1, 2 * d)
    s["w_out"] = w_out.astype(BF16)
    s["w_ple"] = w_ple.astype(BF16)
    s["w_pg"] = w_pg.astype(BF16)
    return s


def _layer_weights(l, dims, b_if, g_mh, g_cq, w_uq, g_ckv, w_ukv, g_q, g_k):
    nh, dv = dims["nh"], dims["dv"]
    ah, ql, kvl, nope, rope, av = dims["ah"], dims["ql"], dims["kvl"], dims["nope"], dims["rope"], dims["av"]
    w = {}
    w["b_col"] = b_if[l].reshape(1, 2 * nh)
    w["b_row"] = b_if[l].reshape(2 * nh, 1)
    w["gmh"] = g_mh[l].reshape(1, nh * dv)
    w["gcq"] = g_cq[l].reshape(1, ql)
    w["gckv"] = g_ckv[l].reshape(1, kvl)
    zpad = jnp.zeros((rope,), F32)
    w["gq"] = jnp.concatenate([g_q[l], zpad]).reshape(1, 2 * LANE)
    w["gk"] = jnp.concatenate([g_k[l], zpad]).reshape(1, 2 * LANE)
    uq = w_uq[l]
    pe = uq[:, :, nope:]
    w["wq"] = jnp.concatenate([uq[:, :, :nope], pe, _rot_cols(pe)], axis=-1).reshape(ql, ah * 2 * LANE).astype(BF16)
    ukv = w_ukv[l]
    w["wkv"] = ukv.reshape(kvl, ah * (nope + av)).astype(BF16)
    w["wukt"] = ukv[:, :, :nope].reshape(kvl, ah * nope).T.astype(BF16)
    w["wuv"] = jnp.transpose(ukv[:, :, nope:], (1, 0, 2)).astype(BF16)
    return w


def _rope_table(pos, rope):
    half = rope // 2
    inv = ROPE_THETA ** (-jnp.arange(half, dtype=F32) / half)
    ang = pos.astype(F32)[:, None] * inv[None, :]
    cos, sin = jnp.cos(ang), jnp.sin(ang)
    return jnp.concatenate([cos, cos, sin, sin], axis=1)


def kernel(x_prompt, x_sample, cache_ckv, cache_kpe, state_C, state_n, state_m, page_table,
           p_prompt, p_sample, norm_g, w_ffn_up, w_ffn_down, w_in, b_if, g_mh, g_cq, w_uq,
           g_ckv, w_ukv, g_q, g_k, w_gate, b_gate, w_out, w_ple, w_pg):
    bsz, seq, d = x_prompt.shape
    db, dseq, _ = x_sample.shape
    assert dseq == 1
    depth, _, nh, dqk, dv = state_C.shape
    _, ql, ah, qk_head = w_uq.shape
    kvl = w_ukv.shape[1]
    rope = cache_kpe.shape[-1]
    nope = qk_head - rope
    av = w_ukv.shape[-1] - nope
    dff = w_ffn_down.shape[2]
    page = cache_ckv.shape[2]
    n_pages = page_table.shape[1]
    past = n_pages * page
    pd = p_prompt.shape[-1]
    col_cq = 2 * nh * dqk + nh * dv + d
    n_used = col_cq + ql + kvl + 2 * rope + 2 * nh
    col_gate = col_cq + ql + kvl + 2 * rope
    tn_in = 1536 if d >= 1536 else 256
    dims = dict(d=d, dff=dff, fp=-(-dff // 512) * 512 if dff > 512 else dff, nh=nh, dqk=dqk, dv=dv, ah=ah,
                ql=ql, kvl=kvl, nope=nope, rope=rope, av=av, nin=-(-n_used // tn_in) * tn_in)
    cl = _tile(seq, 256)
    nc = seq // cl

    cs_p = _rope_table(jnp.arange(seq), rope)
    cs_s = jnp.broadcast_to(_rope_table(past + jnp.arange(1), rope), (db, 2 * rope))
    pt_flat = page_table.reshape(-1).astype(jnp.int32)
    cache_kpe_t = jnp.swapaxes(cache_kpe, 2, 3)

    xp = x_prompt.reshape(bsz * seq, d)
    xs = x_sample.reshape(db, d)
    pp = p_prompt.reshape(depth, bsz * seq, pd)
    ps = p_sample.reshape(depth, db, pd)
    outs_p, outs_s = [], []
    c_all = None
    sw = _stacked_weights(dims, norm_g, w_ffn_up, w_ffn_down, w_in, w_gate, b_gate, w_out, w_ple, w_pg)
    gains = sw["gains"]
    for l in range(depth):
        w = _layer_weights(l, dims, b_if, g_mh, g_cq, w_uq, g_ckv, w_ukv, g_q, g_k)
        mla = dict(col_cq=col_cq, nheads=ah, nope=nope, rope=rope, av=av)

        xp = ffn(xp, gains, sw["w_up"], sw["w_down"], l, 0)
        z = in_proj(xp, gains, sw["w_in"], l, tn_pref=tn_in)
        z3 = z.reshape(bsz, seq, -1)
        gates = z3[:, :, col_gate:col_gate + 2 * nh]
        gates_r = jnp.swapaxes(gates.reshape(bsz, nc, cl, 2 * nh), 2, 3)
        ym, c_p, n_p, m_p = mlstm_prompt(z3, gates, gates_r, w["b_col"], w["b_row"], w["gmh"],
                                         nh=nh, dqk=dqk, dv=dv, cl=cl)
        qh, kh, vh, ckv_p, kpe_p = mla_prep(z3, cs_p, w["gcq"], w["gckv"], w["gq"], w["gk"],
                                            w["wq"], w["wkv"], **mla)
        ya = flash_attention(qh, kh, vh)
        xp = merge(xp, gains, sw["w_gate"], sw["b_gate"], ym.reshape(bsz * seq, d), ya.reshape(bsz * seq, d),
                   sw["w_out"], l)
        xp = ffn(xp, gains, sw["w_up"], sw["w_down"], l, 1, (sw["w_pg"], pp, sw["w_ple"]))
        outs_p.append((ckv_p, kpe_p, c_p, n_p, m_p[:, :, 0]))

        xs = ffn(xs, gains, sw["w_up"], sw["w_down"], l, 0)
        zs = in_proj(xs, gains, sw["w_in"], l, tn_pref=tn_in)
        zs3 = zs.reshape(db, 1, -1)
        ym_s, c_all, n_s, m_s = mlstm_step(zs3, zs3[:, :, col_gate:col_gate + 2 * nh], w["b_col"], w["gmh"],
                                           state_C, state_n[l], state_m[l], l, c_all, nh=nh, dqk=dqk, dv=dv)
        qa, qp, self_logit, v_self, ckv_s, kpe_s = sample_prep(
            zs, cs_s, w["gcq"], w["gckv"], w["gq"], w["gk"], w["wq"], w["wkv"], w["wukt"], **mla)
        acc, m_run, l_run = paged_decode(pt_flat, jnp.swapaxes(qa, 0, 1), jnp.swapaxes(qp, 0, 1), w["wukt"],
                                         cache_ckv, cache_kpe_t, l, n_pages=n_pages, nheads=ah,
                                         nope=nope, rope=rope)
        ya_s = decode_finish(jnp.swapaxes(acc, 0, 1), jnp.swapaxes(m_run, 0, 1), jnp.swapaxes(l_run, 0, 1),
                             self_logit, v_self, w["wuv"])
        xs = merge(xs, gains, sw["w_gate"], sw["b_gate"], ym_s.reshape(db, d), ya_s, sw["w_out"], l)
        xs = ffn(xs, gains, sw["w_up"], sw["w_down"], l, 1, (sw["w_pg"], ps, sw["w_ple"]))
        outs_s.append((ckv_s.reshape(db, 1, kvl), kpe_s.reshape(db, 1, rope), None, n_s, m_s.reshape(db, nh)))

    stack = lambda outs, i: jnp.stack([o[i] for o in outs])
    return (xp.reshape(bsz, seq, d), xs.reshape(db, 1, d),
            stack(outs_p, 0), stack(outs_p, 1), stack(outs_p, 2), stack(outs_p, 3), stack(outs_p, 4),
            stack(outs_s, 0), stack(outs_s, 1), c_all, stack(outs_s, 3), stack(outs_s, 4))
```

```python
import functools
import math

import jax
import jax.numpy as jnp
from jax import lax
from jax.experimental import pallas as pl
from jax.experimental.pallas import tpu as pltpu

F32 = jnp.float32
BF16 = jnp.bfloat16
EPS = 1e-6
ROPE_THETA = 10000.0
LANE = 128
VMEM_LIMIT_BYTES = 56 * 2**20
_NT = (((1,), (1,)), ((), ()))


def _cparams(*sem):
    return pltpu.CompilerParams(dimension_semantics=sem, vmem_limit_bytes=VMEM_LIMIT_BYTES)


def _rms(x, g):
    return x * lax.rsqrt(jnp.mean(x * x, axis=-1, keepdims=True) + EPS) * g


def _log_sigmoid(x):
    return jnp.minimum(x, 0.0) - jnp.log(1.0 + jnp.exp(-jnp.abs(x)))


def _bdot(a, b):
    return jnp.dot(a, b, preferred_element_type=F32)


def _tile(n, pref):
    if n <= pref:
        return n
    t = pref
    while n % t:
        t -= 8
    return t


def _ffn_body(x_ref, g_ref, wa_ref, wb_ref, wd_ref, *rest, with_ple):
    o_ref, xn_ref, acc_ref = rest[-3:]
    j = pl.program_id(1)

    @pl.when(j == 0)
    def _():
        xn_ref[...] = _rms(x_ref[...], g_ref[...]).astype(BF16)
        acc_ref[...] = jnp.zeros_like(acc_ref)

    xn = xn_ref[...]
    a = _bdot(xn, wa_ref[...])
    b = _bdot(xn, wb_ref[...])
    h = (a * jax.nn.sigmoid(a) * b).astype(BF16)
    acc_ref[...] += _bdot(h, wd_ref[...])

    @pl.when(j == pl.num_programs(1) - 1)
    def _():
        y = x_ref[...] + 0.5 * acc_ref[...]
        if with_ple:
            g2_ref, wpg_ref, p_ref, wple_ref = rest[:4]
            gate = jax.nn.sigmoid(_bdot(_rms(y, g2_ref[...]).astype(BF16), wpg_ref[...]))
            y = y + gate * _bdot(p_ref[...].astype(BF16), wple_ref[...])
        o_ref[...] = y


def _norm_spec(d, layer, k):
    return pl.BlockSpec((None, None, 1, d), lambda i, j: (layer, k, 0, 0))


def ffn(x, gains, w_up, w_down, layer, idx, ple=None, *, tm_pref=512, tf_pref=512):
    m, d = x.shape
    fp = w_down.shape[2]
    tm, tf = _tile(m, tm_pref), _tile(fp, tf_pref)
    in_specs = [
        pl.BlockSpec((tm, d), lambda i, j: (i, 0)),
        _norm_spec(d, layer, 2 * idx),
        pl.BlockSpec((None, None, d, tf), lambda i, j: (layer, idx, 0, j)),
        pl.BlockSpec((None, None, d, tf), lambda i, j: (layer, idx, 0, j)),
        pl.BlockSpec((None, None, tf, d), lambda i, j: (layer, idx, j, 0)),
    ]
    args = [x, gains, w_up[0], w_up[1], w_down]
    if ple is not None:
        wpg, p, wple = ple
        pd = p.shape[-1]
        in_specs += [
            _norm_spec(d, layer, 3),
            pl.BlockSpec((None, d, d), lambda i, j: (layer, 0, 0), pipeline_mode=pl.Buffered(1)),
            pl.BlockSpec((None, tm, pd), lambda i, j: (layer, i, 0)),
            pl.BlockSpec((None, pd, d), lambda i, j: (layer, 0, 0), pipeline_mode=pl.Buffered(1)),
        ]
        args += [gains, wpg, p, wple]
    return pl.pallas_call(
        functools.partial(_ffn_body, with_ple=ple is not None),
        grid=(m // tm, fp // tf),
        in_specs=in_specs,
        out_specs=pl.BlockSpec((tm, d), lambda i, j: (i, 0)),
        out_shape=jax.ShapeDtypeStruct((m, d), F32),
        scratch_shapes=[pltpu.VMEM((tm, d), BF16), pltpu.VMEM((tm, d), F32)],
        compiler_params=_cparams("parallel", "arbitrary"),
        name="ffn_ple" if ple is not None else "ffn",
    )(*args)


def _in_proj_body(x_ref, g_ref, w_ref, o_ref, xn_ref):
    @pl.when(pl.program_id(1) == 0)
    def _():
        xn_ref[...] = _rms(x_ref[...], g_ref[...]).astype(BF16)

    o_ref[...] = _bdot(xn_ref[...], w_ref[...])


def in_proj(x, gains, w, layer, *, tm_pref=512, tn_pref=1536):
    m, d = x.shape
    n = w.shape[-1]
    tm, tn = _tile(m, tm_pref), _tile(n, tn_pref)
    return pl.pallas_call(
        _in_proj_body,
        grid=(m // tm, n // tn),
        in_specs=[
            pl.BlockSpec((tm, d), lambda i, j: (i, 0)),
            _norm_spec(d, layer, 1),
            pl.BlockSpec((None, d, tn), lambda i, j: (layer, 0, j)),
        ],
        out_specs=pl.BlockSpec((tm, tn), lambda i, j: (i, j)),
        out_shape=jax.ShapeDtypeStruct((m, n), F32),
        scratch_shapes=[pltpu.VMEM((tm, d), BF16)],
        compiler_params=_cparams("parallel", "arbitrary"),
        name="in_proj",
    )(x, gains, w)


def _merge_body(x_ref, g_ref, wgm_ref, wga_ref, bm_ref, ba_ref, ym_ref, ya_ref, wo_ref,
                o_ref, xn_ref, mix_ref, *, tn):
    j = pl.program_id(1)

    @pl.when(j == 0)
    def _():
        xn_ref[...] = _rms(x_ref[...], g_ref[...]).astype(BF16)

    xn = xn_ref[...]
    gm = jax.nn.sigmoid(_bdot(xn, wgm_ref[...]) + bm_ref[...])
    ga = jax.nn.sigmoid(_bdot(xn, wga_ref[...]) + ba_ref[...])
    col = pl.multiple_of(j * tn, LANE)
    mix_ref[:, pl.ds(col, tn)] = (gm * ym_ref[...] + ga * ya_ref[...]).astype(BF16)

    @pl.when(j == pl.num_programs(1) - 1)
    def _():
        o_ref[...] = x_ref[...] + _bdot(mix_ref[...], wo_ref[...])


def merge(x, gains, w_gate, b_gate, ym, ya, w_out, layer, *, tm_pref=512, tn_pref=512):
    m, d = x.shape
    tm, tn = _tile(m, tm_pref), _tile(d, tn_pref)
    nj = d // tn
    return pl.pallas_call(
        functools.partial(_merge_body, tn=tn),
        grid=(m // tm, nj),
        in_specs=[
            pl.BlockSpec((tm, d), lambda i, j: (i, 0)),
            _norm_spec(d, layer, 1),
            pl.BlockSpec((None, d, tn), lambda i, j: (layer, 0, j)),
            pl.BlockSpec((None, d, tn), lambda i, j: (layer, 0, nj + j)),
            pl.BlockSpec((None, 1, tn), lambda i, j: (layer, 0, j)),
            pl.BlockSpec((None, 1, tn), lambda i, j: (layer, 0, nj + j)),
            pl.BlockSpec((tm, tn), lambda i, j: (i, j)),
            pl.BlockSpec((tm, tn), lambda i, j: (i, j)),
            pl.BlockSpec((None, d, d), lambda i, j: (layer, 0, 0), pipeline_mode=pl.Buffered(1)),
        ],
        out_specs=pl.BlockSpec((tm, d), lambda i, j: (i, 0)),
        out_shape=jax.ShapeDtypeStruct((m, d), F32),
        scratch_shapes=[pltpu.VMEM((tm, d), BF16), pltpu.VMEM((tm, d), BF16)],
        compiler_params=_cparams("parallel", "arbitrary"),
        name="merge",
    )(x, gains, w_gate, w_gate, b_gate, b_gate, ym, ya, w_out)


def _mlstm_chunk_body(q_ref, k_ref, v_ref, zo_ref, gc_ref, gr_ref, bc_ref, br_ref, gmh_ref,
                      ym_ref, c_out, n_out, m_out, c_scr, n_scr, m_scr, *, nh, dqk, dv, cl):
    c = pl.program_id(1)

    @pl.when(c == 0)
    def _():
        c_scr[...] = jnp.zeros_like(c_scr)
        n_scr[...] = jnp.zeros_like(n_scr)
        m_scr[...] = jnp.zeros_like(m_scr)

    gc = gc_ref[0] + bc_ref[...]
    gr = gr_ref[0, 0] + br_ref[...]
    t_idx = lax.broadcasted_iota(jnp.int32, (cl, cl), 0)
    j_idx = lax.broadcasted_iota(jnp.int32, (cl, cl), 1)
    causal = j_idx <= t_idx
    for h in range(nh):
        i_col = gc[:, h:h + 1]
        f_col = _log_sigmoid(gc[:, nh + h:nh + h + 1])
        i_row = gr[h:h + 1, :]
        f_row = _log_sigmoid(gr[nh + h:nh + h + 1, :])
        b_col = jnp.sum(jnp.where(causal, f_row, 0.0), axis=1, keepdims=True)
        b_row = jnp.sum(jnp.where(t_idx <= j_idx, f_col, 0.0), axis=0, keepdims=True)
        m_prev = m_scr[h:h + 1, 0:1]
        d = jnp.where(causal, b_col - b_row + i_row, -jnp.inf)
        inter = b_col + m_prev
        m_t = jnp.maximum(inter, jnp.max(d, axis=1, keepdims=True))
        w_inter = jnp.exp(inter - m_t)
        qh = q_ref[0, :, h * dqk:(h + 1) * dqk]
        kh = k_ref[0, :, h * dqk:(h + 1) * dqk]
        vb = v_ref[0, :, h * dv:(h + 1) * dv].astype(BF16)
        qb = qh.astype(BF16)
        s = lax.dot_general(qb, kh.astype(BF16), _NT, preferred_element_type=F32) * jnp.exp(d - m_t)
        c_old = c_scr[h]
        n_old = n_scr[h:h + 1, :]
        num = w_inter * _bdot(qb, c_old.astype(BF16)) + _bdot(s.astype(BF16), vb)
        den = w_inter * jnp.sum(qh * n_old, axis=1, keepdims=True) + jnp.sum(s, axis=1, keepdims=True)
        hh = num / jnp.maximum(jnp.abs(den), jnp.exp(-m_t))
        hn = _rms(hh, gmh_ref[:, h * dv:(h + 1) * dv])
        ym_ref[0, :, h * dv:(h + 1) * dv] = jax.nn.sigmoid(zo_ref[0, :, h * dv:(h + 1) * dv]) * hn
        b_last = b_col[cl - 1:cl, :]
        g_col = b_last - b_col + i_col
        g_row = b_last - b_row + i_row
        m_new = jnp.maximum(b_last + m_prev, jnp.max(g_row, axis=1, keepdims=True))
        a_prev = jnp.exp(b_last + m_prev - m_new)
        ka = kh * jnp.exp(g_col - m_new)
        c_scr[h] = a_prev * c_old + _bdot(ka.T.astype(BF16), vb)
        n_scr[h:h + 1, :] = a_prev * n_old + jnp.sum(ka, axis=0, keepdims=True)
        m_scr[h:h + 1, :] = jnp.broadcast_to(m_new, (1, LANE))

    @pl.when(c == pl.num_programs(1) - 1)
    def _():
        c_out[0] = c_scr[...]
        n_out[0] = n_scr[...]
        m_out[0] = m_scr[...]


def mlstm_prompt(z3, gc, gr, bc, br, gmh, *, nh, dqk, dv, cl):
    bsz, s, _ = z3.shape
    nc = s // cl
    wq, wv = nh * dqk, nh * dv
    assert wq % LANE == 0 and wv % wq == 0
    kq, kv = wq // wq, wv // wq
    v_blk = (2 * wq) // wv
    assert (2 * wq) % wv == 0
    body = functools.partial(_mlstm_chunk_body, nh=nh, dqk=dqk, dv=dv, cl=cl)
    return pl.pallas_call(
        body,
        grid=(bsz, nc),
        in_specs=[
            pl.BlockSpec((1, cl, wq), lambda b, c: (b, c, 0)),
            pl.BlockSpec((1, cl, wq), lambda b, c: (b, c, kq)),
            pl.BlockSpec((1, cl, wv), lambda b, c: (b, c, v_blk)),
            pl.BlockSpec((1, cl, wv), lambda b, c: (b, c, v_blk + 1)),
            pl.BlockSpec((1, cl, 2 * nh), lambda b, c: (b, c, 0)),
            pl.BlockSpec((1, 1, 2 * nh, cl), lambda b, c: (b, c, 0, 0)),
            pl.BlockSpec((1, 2 * nh), lambda b, c: (0, 0)),
            pl.BlockSpec((2 * nh, 1), lambda b, c: (0, 0)),
            pl.BlockSpec((1, wv), lambda b, c: (0, 0)),
        ],
        out_specs=[
            pl.BlockSpec((1, cl, wv), lambda b, c: (b, c, 0)),
            pl.BlockSpec((1, nh, dqk, dv), lambda b, c: (b, 0, 0, 0)),
            pl.BlockSpec((1, nh, dqk), lambda b, c: (b, 0, 0)),
            pl.BlockSpec((1, nh, LANE), lambda b, c: (b, 0, 0)),
        ],
        out_shape=[
            jax.ShapeDtypeStruct((bsz, s, wv), F32),
            jax.ShapeDtypeStruct((bsz, nh, dqk, dv), F32),
            jax.ShapeDtypeStruct((bsz, nh, dqk), F32),
            jax.ShapeDtypeStruct((bsz, nh, LANE), F32),
        ],
        scratch_shapes=[pltpu.VMEM((nh, dqk, dv), F32), pltpu.VMEM((nh, dqk), F32),
                        pltpu.VMEM((nh, LANE), F32)],
        compiler_params=_cparams("parallel", "arbitrary"),
        name="mlstm_prompt",
    )(z3, z3, z3, z3, gc, gr, bc, br, gmh)


def _mlstm_step_body(q_ref, k_ref, v_ref, zo_ref, gs_ref, bias_ref, gmh_ref, c_ref, n_ref, m_ref, *rest,
                     nh, dqk, dv):
    ym_ref, c_out, n_out, m_out = rest[-4:]
    c_ref, c_out = c_ref.at[0], c_out.at[0]
    gates = gs_ref[0] + bias_ref[...]
    eye = (lax.broadcasted_iota(jnp.int32, (dqk, dqk), 0)
           == lax.broadcasted_iota(jnp.int32, (dqk, dqk), 1))

    def as_column(row):
        return jnp.sum(jnp.where(eye, row, 0.0), axis=1, keepdims=True)

    for h in range(nh):
        i_pre = gates[:, h:h + 1]
        logf = _log_sigmoid(gates[:, nh + h:nh + h + 1])
        m_prev = m_ref[0, :, h:h + 1]
        inter = logf + m_prev
        m_t = jnp.maximum(inter, i_pre)
        a_prev = jnp.exp(inter - m_t)
        a_new = jnp.exp(i_pre - m_t)
        qh = q_ref[0, :, h * dqk:(h + 1) * dqk]
        kh = k_ref[0, :, h * dqk:(h + 1) * dqk] * a_new
        vh = v_ref[0, :, h * dv:(h + 1) * dv]
        c_new = a_prev * c_ref[0, h] + as_column(kh) * vh
        n_new = a_prev * n_ref[0, h:h + 1, :] + kh
        num = jnp.sum(as_column(qh) * c_new, axis=0, keepdims=True)
        den = jnp.sum(qh * n_new, axis=1, keepdims=True)
        hh = num / jnp.maximum(jnp.abs(den), jnp.exp(-m_t))
        hn = _rms(hh, gmh_ref[:, h * dv:(h + 1) * dv])
        ym_ref[0, :, h * dv:(h + 1) * dv] = jax.nn.sigmoid(zo_ref[0, :, h * dv:(h + 1) * dv]) * hn
        c_out[0, h] = c_new
        n_out[0, h:h + 1, :] = n_new
        m_out[0, :, h:h + 1] = m_t


def mlstm_step(z3, gs, bias, gmh, state_c, n0, m0, layer, c_all=None, *, nh, dqk, dv):
    db = z3.shape[0]
    wq, wv = nh * dqk, nh * dv
    v_blk = (2 * wq) // wv
    body = functools.partial(_mlstm_step_body, nh=nh, dqk=dqk, dv=dv)
    m3 = m0.reshape(db, 1, nh)
    c_spec = pl.BlockSpec((1, 1, nh, dqk, dv), lambda s: (layer, s, 0, 0, 0))
    in_specs = [
        pl.BlockSpec((1, 1, wq), lambda s: (s, 0, 0)),
        pl.BlockSpec((1, 1, wq), lambda s: (s, 0, 1)),
        pl.BlockSpec((1, 1, wv), lambda s: (s, 0, v_blk)),
        pl.BlockSpec((1, 1, wv), lambda s: (s, 0, v_blk + 1)),
        pl.BlockSpec((1, 1, 2 * nh), lambda s: (s, 0, 0)),
        pl.BlockSpec((1, 2 * nh), lambda s: (0, 0)),
        pl.BlockSpec((1, wv), lambda s: (0, 0)),
        c_spec,
        pl.BlockSpec((1, nh, dqk), lambda s: (s, 0, 0)),
        pl.BlockSpec((1, 1, nh), lambda s: (s, 0, 0)),
    ]
    args = [z3, z3, z3, z3, gs, bias, gmh, state_c, n0, m3]
    aliases = {}
    if c_all is not None:
        in_specs.append(pl.BlockSpec(memory_space=pl.ANY))
        args.append(c_all)
        aliases = {len(args) - 1: 1}
    return pl.pallas_call(
        body,
        grid=(db,),
        in_specs=in_specs,
        out_specs=[
            pl.BlockSpec((1, 1, wv), lambda s: (s, 0, 0)),
            c_spec,
            pl.BlockSpec((1, nh, dqk), lambda s: (s, 0, 0)),
            pl.BlockSpec((1, 1, nh), lambda s: (s, 0, 0)),
        ],
        out_shape=[
            jax.ShapeDtypeStruct((db, 1, wv), F32),
            jax.ShapeDtypeStruct(state_c.shape, F32),
            jax.ShapeDtypeStruct((db, nh, dqk), F32),
            jax.ShapeDtypeStruct((db, 1, nh), F32),
        ],
        input_output_aliases=aliases,
        compiler_params=_cparams("parallel"),
        name="mlstm_step",
    )(*args)


def _rope_half(xb, cs):
    t = xb * cs
    return t + pltpu.roll(t, 64, axis=1)


def _query_heads(cq_b, wq_ref, cs, gq_ref, h, nope, qk_head, low):
    w = 2 * LANE
    qf = _bdot(cq_b, wq_ref[:, h * w:(h + 1) * w])
    qn = qf[:, :nope]
    qp = jnp.where(low, _rope_half(qf[:, nope:], cs), 0.0)
    ssq = jnp.sum(qn * qn, axis=1, keepdims=True) + jnp.sum(qp * qp, axis=1, keepdims=True)
    r = lax.rsqrt(ssq / qk_head + EPS) * (qk_head ** -0.5)
    return qn * r * gq_ref[:, :nope], qp * r * gq_ref[:, nope:]


def _key_heads(ckv_b, wkv_ref, kpe, kpe_sq, gk_ref, h, nope, qk_head):
    w = 2 * LANE
    kvf = _bdot(ckv_b, wkv_ref[:, h * w:(h + 1) * w])
    kn = kvf[:, :nope]
    r = lax.rsqrt((jnp.sum(kn * kn, axis=1, keepdims=True) + kpe_sq) / qk_head + EPS)
    return kn * r * gk_ref[:, :nope], kpe * r * gk_ref[:, nope:], kvf[:, nope:]


def _mla_prep_body(zcq_ref, zckv_ref, zkr_ref, cs_ref, gcq_ref, gckv_ref, gq_ref, gk_ref,
                   wq_ref, wkv_ref, q_out, k_out, v_out, ckv_out, kpe_out, *, nheads, nope, rope):
    qk_head = nope + rope
    cs = cs_ref[...]
    tm = cs.shape[0]
    low = lax.broadcasted_iota(jnp.int32, (tm, LANE), 1) < rope
    cq_b = _rms(zcq_ref[0], gcq_ref[...]).astype(BF16)
    ckv = _rms(zckv_ref[0], gckv_ref[...])
    ckv_out[0] = ckv
    ckv_b = ckv.astype(BF16)
    kpe = jnp.where(low, _rope_half(zkr_ref[0], cs), 0.0)
    kpe_out[0] = kpe[:, :rope]
    kpe_sq = jnp.sum(kpe * kpe, axis=1, keepdims=True)
    for h in range(nheads):
        qn, qp = _query_heads(cq_b, wq_ref, cs, gq_ref, h, nope, qk_head, low)
        q_out[0, h, :, :nope] = qn.astype(BF16)
        q_out[0, h, :, nope:] = qp.astype(BF16)
        kn, kp, v = _key_heads(ckv_b, wkv_ref, kpe, kpe_sq, gk_ref, h, nope, qk_head)
        k_out[0, h, :, :nope] = kn.astype(BF16)
        k_out[0, h, :, nope:] = kp.astype(BF16)
        v_out[0, h, :, :nope] = v.astype(BF16)
        v_out[0, h, :, nope:] = jnp.ones_like(v, BF16)


def mla_prep(z3, cs, gcq, gckv, gq, gk, wq, wkv, *, col_cq, nheads, nope, rope, av, tm_pref=256):
    bsz, s, _ = z3.shape
    ql, kvl = wq.shape[0], wkv.shape[0]
    assert nope == LANE and 2 * rope == LANE and av == LANE and ql == kvl
    tm = _tile(s, tm_pref)
    cb = col_cq // ql
    kr_blk = (col_cq + ql + kvl) // LANE
    body = functools.partial(_mla_prep_body, nheads=nheads, nope=nope, rope=rope)
    hw = 2 * LANE
    return pl.pallas_call(
        body,
        grid=(bsz, s // tm),
        in_specs=[
            pl.BlockSpec((1, tm, ql), lambda b, i: (b, i, cb)),
            pl.BlockSpec((1, tm, kvl), lambda b, i: (b, i, cb + 1)),
            pl.BlockSpec((1, tm, LANE), lambda b, i: (b, i, kr_blk)),
            pl.BlockSpec((tm, LANE), lambda b, i: (i, 0)),
            pl.BlockSpec((1, ql), lambda b, i: (0, 0)),
            pl.BlockSpec((1, kvl), lambda b, i: (0, 0)),
            pl.BlockSpec((1, hw), lambda b, i: (0, 0)),
            pl.BlockSpec((1, hw), lambda b, i: (0, 0)),
            pl.BlockSpec((ql, nheads * hw), lambda b, i: (0, 0)),
            pl.BlockSpec((kvl, nheads * hw), lambda b, i: (0, 0)),
        ],
        out_specs=[
            pl.BlockSpec((1, nheads, tm, hw), lambda b, i: (b, 0, i, 0)),
            pl.BlockSpec((1, nheads, tm, hw), lambda b, i: (b, 0, i, 0)),
            pl.BlockSpec((1, nheads, tm, 2 * av), lambda b, i: (b, 0, i, 0)),
            pl.BlockSpec((1, tm, kvl), lambda b, i: (b, i, 0)),
            pl.BlockSpec((1, tm, rope), lambda b, i: (b, i, 0)),
        ],
        out_shape=[
            jax.ShapeDtypeStruct((bsz, nheads, s, hw), BF16),
            jax.ShapeDtypeStruct((bsz, nheads, s, hw), BF16),
            jax.ShapeDtypeStruct((bsz, nheads, s, 2 * av), BF16),
            jax.ShapeDtypeStruct((bsz, s, kvl), F32),
            jax.ShapeDtypeStruct((bsz, s, rope), F32),
        ],
        compiler_params=_cparams("parallel", "parallel"),
        name="mla_prep",
    )(z3, z3, z3, cs, gcq, gckv, gq, gk, wq, wkv)


def _flash_body(q_ref, k_ref, v_ref, o_ref, s_scr, m_scr, acc_scr, *, tq, tk, av):
    qi = pl.program_id(2)
    m_scr[...] = jnp.full_like(m_scr, -jnp.inf)
    acc_scr[...] = jnp.zeros_like(acc_scr)
    q = q_ref[0, 0]

    def scores(ki, slot, r0=0):
        off = pl.multiple_of(ki * tk, tk)
        s_scr[slot, r0:, :] = lax.dot_general(q[r0:], k_ref[0, 0, pl.ds(off, tk), :], _NT,
                                              preferred_element_type=F32)

    def update(ki, slot, r0=0, r1=tq, triangular=False):
        off = pl.multiple_of(ki * tk, tk)
        s = s_scr[slot, r0:r1, :]
        if triangular:
            row = lax.broadcasted_iota(jnp.int32, (tk, tk), 0)
            col = lax.broadcasted_iota(jnp.int32, (tk, tk), 1)
            s = jnp.where(col <= row, s, -jnp.inf)
        m_old = m_scr[r0:r1]
        m_new = jnp.maximum(m_old, jnp.max(s, axis=1, keepdims=True))
        alpha = jnp.exp(m_old - m_new)
        p = jnp.exp(s - m_new)
        acc_scr[r0:r1] = alpha * acc_scr[r0:r1] + _bdot(p.astype(BF16), v_ref[0, 0, pl.ds(off, tk), :])
        m_scr[r0:r1] = m_new

    scores(0, 0)

    def pair(i, carry):
        scores(2 * i + 1, 1)
        update(2 * i, 0)
        scores(2 * i + 2, 0)
        update(2 * i + 1, 1)
        return carry

    lax.fori_loop(0, qi, pair, 0)
    scores(2 * qi + 1, 1, tk)
    update(2 * qi, 0, 0, tk, triangular=True)
    update(2 * qi, 0, tk, tq)
    update(2 * qi + 1, 1, tk, tq, triangular=True)
    o_ref[0] = acc_scr[:, :av] / acc_scr[:, av:]


def flash_attention(q, k, v, *, tq_pref=1024):
    bsz, nheads, s, hw = q.shape
    av = v.shape[-1] // 2
    tq = _tile(s, tq_pref)
    tk = tq // 2
    body = functools.partial(_flash_body, tq=tq, tk=tk, av=av)
    return pl.pallas_call(
        body,
        grid=(bsz, nheads, s // tq),
        in_specs=[
            pl.BlockSpec((1, 1, tq, hw), lambda b, h, i: (b, h, i, 0)),
            pl.BlockSpec((1, 1, s, hw), lambda b, h, i: (b, h, 0, 0)),
            pl.BlockSpec((1, 1, s, 2 * av), lambda b, h, i: (b, h, 0, 0)),
        ],
        out_specs=pl.BlockSpec((1, tq, av), lambda b, h, i: (b, i, h)),
        out_shape=jax.ShapeDtypeStruct((bsz, s, nheads * av), F32),
        scratch_shapes=[pltpu.VMEM((2, tq, tk), F32), pltpu.VMEM((tq, 1), F32),
                        pltpu.VMEM((tq, 2 * av), F32)],
        compiler_params=_cparams("parallel", "parallel", "arbitrary"),
        name="flash_attention",
    )(q, k, v)


def _sample_prep_body(zcq_ref, zckv_ref, zkr_ref, cs_ref, gcq_ref, gckv_ref, gq_ref, gk_ref,
                      wq_ref, wkv_ref, wukt_ref, qa_out, qp_out, self_out, v_out, ckv_out, kpe_out,
                      *, nheads, nope, rope):
    qk_head = nope + rope
    cs = cs_ref[...]
    tm = cs.shape[0]
    low = lax.broadcasted_iota(jnp.int32, (tm, LANE), 1) < rope
    cq_b = _rms(zcq_ref[...], gcq_ref[...]).astype(BF16)
    ckv = _rms(zckv_ref[...], gckv_ref[...])
    ckv_out[...] = ckv
    ckv_b = ckv.astype(BF16)
    kpe = jnp.where(low, _rope_half(zkr_ref[...], cs), 0.0)
    kpe_out[...] = kpe[:, :rope]
    kpe_sq = jnp.sum(kpe * kpe, axis=1, keepdims=True)
    for h in range(nheads):
        qn, qp = _query_heads(cq_b, wq_ref, cs, gq_ref, h, nope, qk_head, low)
        kn, kp, v = _key_heads(ckv_b, wkv_ref, kpe, kpe_sq, gk_ref, h, nope, qk_head)
        self_out[h] = jnp.broadcast_to(
            jnp.sum(qn * kn, axis=1, keepdims=True) + jnp.sum(qp * kp, axis=1, keepdims=True), (tm, LANE))
        v_out[h] = v
        qg = (qn * gk_ref[:, :nope]).astype(BF16)
        qa_out[h] = _bdot(qg, wukt_ref[h * nope:(h + 1) * nope, :]).astype(BF16)
        qp_out[h] = (qp * gk_ref[:, nope:])[:, :rope].astype(BF16)


def sample_prep(z, cs, gcq, gckv, gq, gk, wq, wkv, wukt, *, col_cq, nheads, nope, rope, av):
    db = z.shape[0]
    ql, kvl = wq.shape[0], wkv.shape[0]
    cb = col_cq // ql
    kr_blk = (col_cq + ql + kvl) // LANE
    hw = 2 * LANE
    body = functools.partial(_sample_prep_body, nheads=nheads, nope=nope, rope=rope)
    full = lambda shape: pl.BlockSpec(shape, lambda i: (0,) * len(shape))
    return pl.pallas_call(
        body,
        grid=(1,),
        in_specs=[
            pl.BlockSpec((db, ql), lambda i: (0, cb)),
            pl.BlockSpec((db, kvl), lambda i: (0, cb + 1)),
            pl.BlockSpec((db, LANE), lambda i: (0, kr_blk)),
            full((db, LANE)), full((1, ql)), full((1, kvl)), full((1, hw)), full((1, hw)),
            full((ql, nheads * hw)), full((kvl, nheads * hw)), full((nheads * nope, kvl)),
        ],
        out_specs=[
            full((nheads, db, kvl)), full((nheads, db, rope)), full((nheads, db, LANE)),
            full((nheads, db, av)), full((db, kvl)), full((db, rope)),
        ],
        out_shape=[
            jax.ShapeDtypeStruct((nheads, db, kvl), BF16),
            jax.ShapeDtypeStruct((nheads, db, rope), BF16),
            jax.ShapeDtypeStruct((nheads, db, LANE), F32),
            jax.ShapeDtypeStruct((nheads, db, av), F32),
            jax.ShapeDtypeStruct((db, kvl), F32),
            jax.ShapeDtypeStruct((db, rope), F32),
        ],
        compiler_params=_cparams("arbitrary"),
        name="sample_prep",
    )(z, z, z, cs, gcq, gckv, gq, gk, wq, wkv, wukt)


def _decode_body(pt_ref, qa_ref, qp_ref, wukt_ref, *refs, npg, nheads, nope, rope, page, sub):
    ckv_refs = refs[:npg]
    kpe_refs = refs[npg:2 * npg]
    acc_out, m_out, l_out = refs[2 * npg:2 * npg + 3]
    lhs_scr, cb_scr, r_scr, lg_scr, m_scr, l_scr, acc_scr = refs[2 * npg + 3:]
    del pt_ref
    j = pl.program_id(1)
    qk_head = nope + rope
    nk = nheads * nope
    ppt = sub // page

    @pl.when(j == 0)
    def _():
        lhs_scr[0:nk, :] = wukt_ref[...]
        lhs_scr[nk:nk + nheads, :] = qa_ref[0]
        m_scr[...] = jnp.full_like(m_scr, -jnp.inf)
        l_scr[...] = jnp.zeros_like(l_scr)
        acc_scr[...] = jnp.zeros_like(acc_scr)

    qp = qp_ref[0]
    ntrips = npg // ppt

    def project(t):
        pages = range(t * ppt, (t + 1) * ppt)
        cb = jnp.concatenate([ckv_refs[p][0, 0].astype(BF16) for p in pages], axis=0)
        cb_scr[t * sub:(t + 1) * sub, :] = cb
        r_scr[t % 2] = lax.dot_general(lhs_scr[...], cb, _NT, preferred_element_type=F32)

    def logits_of(t):
        pages = range(t * ppt, (t + 1) * ppt)
        kpt = jnp.concatenate([kpe_refs[p][0, 0] for p in pages], axis=1)
        kn = r_scr[t % 2, 0:nk, :].reshape(nheads, nope, sub)
        ssq = jnp.sum(kn * kn, axis=1)
        pe_sq = jnp.sum(kpt * kpt, axis=0, keepdims=True)
        s_pe = _bdot(qp, kpt.astype(BF16))
        lg_scr[:, t * sub:(t + 1) * sub] = (r_scr[t % 2, nk:nk + nheads, :] + s_pe) * lax.rsqrt(
            (ssq + pe_sq) / qk_head + EPS)

    project(0)
    for t in range(ntrips):
        if t + 1 < ntrips:
            project(t + 1)
        logits_of(t)

    logits = lg_scr[...]
    m_old = m_scr[...]
    m_new = jnp.maximum(m_old, jnp.max(logits, axis=1, keepdims=True))
    alpha = jnp.exp(m_old - m_new)
    pr = jnp.exp(logits - m_new)
    l_scr[...] = alpha * l_scr[...] + jnp.sum(pr, axis=1, keepdims=True)
    acc_scr[...] = alpha * acc_scr[...] + _bdot(pr.astype(BF16), cb_scr[...])
    m_scr[...] = m_new

    @pl.when(j == pl.num_programs(1) - 1)
    def _():
        acc_out[0] = acc_scr[...]
        m_out[0] = jnp.broadcast_to(m_scr[...], (nheads, LANE))
        l_out[0] = jnp.broadcast_to(l_scr[...], (nheads, LANE))


def paged_decode(pt_flat, qa, qp, wukt, cache_ckv, cache_kpe_t, layer, *, n_pages, nheads, nope, rope,
                 npg=32, sub=1024):
    db = qa.shape[0]
    page, kvl = cache_ckv.shape[2], cache_ckv.shape[3]
    npg = math.gcd(npg, n_pages)
    sub = math.gcd(sub, npg * page)
    nk = nheads * nope
    body = functools.partial(_decode_body, npg=npg, nheads=nheads, nope=nope, rope=rope, page=page, sub=sub)

    def page_spec(shape, p):
        return pl.BlockSpec((1, 1) + shape, lambda s, j, pt: (layer, pt[s * n_pages + j * npg + p], 0, 0))

    in_specs = [
        pl.BlockSpec((1, nheads, kvl), lambda s, j, pt: (s, 0, 0)),
        pl.BlockSpec((1, nheads, rope), lambda s, j, pt: (s, 0, 0)),
        pl.BlockSpec((nk, kvl), lambda s, j, pt: (0, 0)),
    ]
    in_specs += [page_spec((page, kvl), p) for p in range(npg)]
    in_specs += [page_spec((rope, page), p) for p in range(npg)]
    out_spec = lambda w: pl.BlockSpec((1, nheads, w), lambda s, j, pt: (s, 0, 0))
    grid_spec = pltpu.PrefetchScalarGridSpec(
        num_scalar_prefetch=1,
        grid=(db, n_pages // npg),
        in_specs=in_specs,
        out_specs=[out_spec(kvl), out_spec(LANE), out_spec(LANE)],
        scratch_shapes=[
            pltpu.VMEM((nk + nheads, kvl), BF16),
            pltpu.VMEM((npg * page, kvl), BF16),
            pltpu.VMEM((2, nk + nheads, sub), F32),
            pltpu.VMEM((nheads, npg * page), F32),
            pltpu.VMEM((nheads, 1), F32),
            pltpu.VMEM((nheads, 1), F32),
            pltpu.VMEM((nheads, kvl), F32),
        ],
    )
    return pl.pallas_call(
        body,
        grid_spec=grid_spec,
        out_shape=[
            jax.ShapeDtypeStruct((db, nheads, kvl), F32),
            jax.ShapeDtypeStruct((db, nheads, LANE), F32),
            jax.ShapeDtypeStruct((db, nheads, LANE), F32),
        ],
        compiler_params=_cparams("arbitrary", "arbitrary"),
        name="paged_decode",
    )(pt_flat, qa, qp, wukt, *([cache_ckv] * npg), *([cache_kpe_t] * npg))


def _decode_finish_body(acc_ref, m_ref, l_ref, self_ref, v_ref, wuv_ref, o_ref, *, nheads, av):
    for h in range(nheads):
        m_old = m_ref[h][:, 0:1]
        ls = self_ref[h][:, 0:1]
        m_new = jnp.maximum(m_old, ls)
        alpha = jnp.exp(m_old - m_new)
        ps = jnp.exp(ls - m_new)
        denom = alpha * l_ref[h][:, 0:1] + ps
        ctx = (acc_ref[h] * alpha).astype(BF16)
        o_ref[:, h * av:(h + 1) * av] = (_bdot(ctx, wuv_ref[h]) + ps * v_ref[h]) / denom


def decode_finish(acc, m, l, self_logit, v_self, wuv):
    nheads, db, kvl = acc.shape
    av = wuv.shape[-1]
    full = lambda shape: pl.BlockSpec(shape, lambda i: (0,) * len(shape))
    return pl.pallas_call(
        functools.partial(_decode_finish_body, nheads=nheads, av=av),
        grid=(1,),
        in_specs=[full(acc.shape), full(m.shape), full(l.shape), full(self_logit.shape),
                  full(v_self.shape), full(wuv.shape)],
        out_specs=full((db, nheads * av)),
        out_shape=jax.ShapeDtypeStruct((db, nheads * av), F32),
        compiler_params=_cparams("arbitrary"),
        name="decode_finish",
    )(acc, m, l, self_logit, v_self, wuv)


def _rot_cols(w):
    half = w.shape[-1] // 2
    return jnp.concatenate([-w[..., half:], w[..., :half]], axis=-1)


def _pad_last(w, n):
    return jnp.pad(w, [(0, 0)] * (w.ndim - 1) + [(0, n - w.shape[-1])])


def _stacked_weights(dims, norm_g, w_ffn_up, w_ffn_down, w_in, w_gate, b_gate, w_out, w_ple, w_pg):
    d, dff, fp, nh, dqk, dv = dims["d"], dims["dff"], dims["fp"], dims["nh"], dims["dqk"], dims["dv"]
    ql, kvl, rope = dims["ql"], dims["kvl"], dims["rope"]
    depth = norm_g.shape[0]
    s = {"gains": norm_g.reshape(depth, 4, 1, d)}
    s["w_up"] = (_pad_last(w_ffn_up[..., :dff], fp).astype(BF16), _pad_last(w_ffn_up[..., dff:], fp).astype(BF16))
    s["w_down"] = jnp.pad(w_ffn_down, ((0, 0), (0, 0), (0, fp - dff), (0, 0))).astype(BF16)
    o = [0]
    for width in (nh * dqk, nh * dqk, nh * dv, d, nh, nh, ql, kvl, rope):
        o.append(o[-1] + width)
    zq, zk, zv, zo, zi, zf, zcq, zckv, zkr = (w_in[:, :, o[i]:o[i + 1]] for i in range(9))
    cols = [zq, zk * (dqk ** -0.5), zv, zo, zcq, zckv, zkr, _rot_cols(zkr), zi, zf]
    s["w_in"] = _pad_last(jnp.concatenate(cols, axis=2), dims["nin"]).astype(BF16)
    s["w_gate"] = w_gate.astype(BF16)
    s["b_gate"] = b_gate.reshape(depth, 1, 2 * d)
    s["w_out"] = w_out.astype(BF16)
    s["w_ple"] = w_ple.astype(BF16)
    s["w_pg"] = w_pg.astype(BF16)
    return s


def _layer_weights(l, dims, b_if, g_mh, g_cq, w_uq, g_ckv, w_ukv, g_q, g_k):
    nh, dv = dims["nh"], dims["dv"]
    ah, ql, kvl, nope, rope, av = dims["ah"], dims["ql"], dims["kvl"], dims["nope"], dims["rope"], dims["av"]
    w = {}
    w["b_col"] = b_if[l].reshape(1, 2 * nh)
    w["b_row"] = b_if[l].reshape(2 * nh, 1)
    w["gmh"] = g_mh[l].reshape(1, nh * dv)
    w["gcq"] = g_cq[l].reshape(1, ql)
    w["gckv"] = g_ckv[l].reshape(1, kvl)
    zpad = jnp.zeros((rope,), F32)
    w["gq"] = jnp.concatenate([g_q[l], zpad]).reshape(1, 2 * LANE)
    w["gk"] = jnp.concatenate([g_k[l], zpad]).reshape(1, 2 * LANE)
    uq = w_uq[l]
    pe = uq[:, :, nope:]
    w["wq"] = jnp.concatenate([uq[:, :, :nope], pe, _rot_cols(pe)], axis=-1).reshape(ql, ah * 2 * LANE).astype(BF16)
    ukv = w_ukv[l]
    w["wkv"] = ukv.reshape(kvl, ah * (nope + av)).astype(BF16)
    w["wukt"] = ukv[:, :, :nope].reshape(kvl, ah * nope).T.astype(BF16)
    w["wuv"] = jnp.transpose(ukv[:, :, nope:], (1, 0, 2)).astype(BF16)
    return w


def _rope_table(pos, rope):
    half = rope // 2
    inv = ROPE_THETA ** (-jnp.arange(half, dtype=F32) / half)
    ang = pos.astype(F32)[:, None] * inv[None, :]
    cos, sin = jnp.cos(ang), jnp.sin(ang)
    return jnp.concatenate([cos, cos, sin, sin], axis=1)


def kernel(x_prompt, x_sample, cache_ckv, cache_kpe, state_C, state_n, state_m, page_table,
           p_prompt, p_sample, norm_g, w_ffn_up, w_ffn_down, w_in, b_if, g_mh, g_cq, w_uq,
           g_ckv, w_ukv, g_q, g_k, w_gate, b_gate, w_out, w_ple, w_pg):
    bsz, seq, d = x_prompt.shape
    db, dseq, _ = x_sample.shape
    assert dseq == 1
    depth, _, nh, dqk, dv = state_C.shape
    _, ql, ah, qk_head = w_uq.shape
    kvl = w_ukv.shape[1]
    rope = cache_kpe.shape[-1]
    nope = qk_head - rope
    av = w_ukv.shape[-1] - nope
    dff = w_ffn_down.shape[2]
    page = cache_ckv.shape[2]
    n_pages = page_table.shape[1]
    past = n_pages * page
    pd = p_prompt.shape[-1]
    col_cq = 2 * nh * dqk + nh * dv + d
    n_used = col_cq + ql + kvl + 2 * rope + 2 * nh
    col_gate = col_cq + ql + kvl + 2 * rope
    tn_in = 1536 if d >= 1536 else 256
    dims = dict(d=d, dff=dff, fp=-(-dff // 512) * 512 if dff > 512 else dff, nh=nh, dqk=dqk, dv=dv, ah=ah,
                ql=ql, kvl=kvl, nope=nope, rope=rope, av=av, nin=-(-n_used // tn_in) * tn_in)
    cl = _tile(seq, 256)
    nc = seq // cl

    cs_p = _rope_table(jnp.arange(seq), rope)
    cs_s = jnp.broadcast_to(_rope_table(past + jnp.arange(1), rope), (db, 2 * rope))
    pt_flat = page_table.reshape(-1).astype(jnp.int32)
    cache_kpe_t = jnp.swapaxes(cache_kpe, 2, 3)

    xp = x_prompt.reshape(bsz * seq, d)
    xs = x_sample.reshape(db, d)
    pp = p_prompt.reshape(depth, bsz * seq, pd)
    ps = p_sample.reshape(depth, db, pd)
    outs_p, outs_s = [], []
    c_all = None
    sw = _stacked_weights(dims, norm_g, w_ffn_up, w_ffn_down, w_in, w_gate, b_gate, w_out, w_ple, w_pg)
    gains = sw["gains"]
    for l in range(depth):
        w = _layer_weights(l, dims, b_if, g_mh, g_cq, w_uq, g_ckv, w_ukv, g_q, g_k)
        mla = dict(col_cq=col_cq, nheads=ah, nope=nope, rope=rope, av=av)

        xp = ffn(xp, gains, sw["w_up"], sw["w_down"], l, 0)
        z = in_proj(xp, gains, sw["w_in"], l, tn_pref=tn_in)
        z3 = z.reshape(bsz, seq, -1)
        gates = z3[:, :, col_gate:col_gate + 2 * nh]
        gates_r = jnp.swapaxes(gates.reshape(bsz, nc, cl, 2 * nh), 2, 3)
        ym, c_p, n_p, m_p = mlstm_prompt(z3, gates, gates_r, w["b_col"], w["b_row"], w["gmh"],
                                         nh=nh, dqk=dqk, dv=dv, cl=cl)
        qh, kh, vh, ckv_p, kpe_p = mla_prep(z3, cs_p, w["gcq"], w["gckv"], w["gq"], w["gk"],
                                            w["wq"], w["wkv"], **mla)
        ya = flash_attention(qh, kh, vh)
        xp = merge(xp, gains, sw["w_gate"], sw["b_gate"], ym.reshape(bsz * seq, d), ya.reshape(bsz * seq, d),
                   sw["w_out"], l)
        xp = ffn(xp, gains, sw["w_up"], sw["w_down"], l, 1, (sw["w_pg"], pp, sw["w_ple"]))
        outs_p.append((ckv_p, kpe_p, c_p, n_p, m_p[:, :, 0]))

        xs = ffn(xs, gains, sw["w_up"], sw["w_down"], l, 0)
        zs = in_proj(xs, gains, sw["w_in"], l, tn_pref=tn_in)
        zs3 = zs.reshape(db, 1, -1)
        ym_s, c_all, n_s, m_s = mlstm_step(zs3, zs3[:, :, col_gate:col_gate + 2 * nh], w["b_col"], w["gmh"],
                                           state_C, state_n[l], state_m[l], l, c_all, nh=nh, dqk=dqk, dv=dv)
        qa, qp, self_logit, v_self, ckv_s, kpe_s = sample_prep(
            zs, cs_s, w["gcq"], w["gckv"], w["gq"], w["gk"], w["wq"], w["wkv"], w["wukt"], **mla)
        acc, m_run, l_run = paged_decode(pt_flat, jnp.swapaxes(qa, 0, 1), jnp.swapaxes(qp, 0, 1), w["wukt"],
                                         cache_ckv, cache_kpe_t, l, n_pages=n_pages, nheads=ah,
                                         nope=nope, rope=rope)
        ya_s = decode_finish(jnp.swapaxes(acc, 0, 1), jnp.swapaxes(m_run, 0, 1), jnp.swapaxes(l_run, 0, 1),
                             self_logit, v_self, w["wuv"])
        xs = merge(xs, gains, sw["w_gate"], sw["b_gate"], ym_s.reshape(db, d), ya_s, sw["w_out"], l)
        xs = ffn(xs, gains, sw["w_up"], sw["w_down"], l, 1, (sw["w_pg"], ps, sw["w_ple"]))
        outs_s.append((ckv_s.reshape(db, 1, kvl), kpe_s.reshape(db, 1, rope), None, n_s, m_s.reshape(db, nh)))

    stack = lambda outs, i: jnp.stack([o[i] for o in outs])
    return (xp.reshape(bsz, seq, d), xs.reshape(db, 1, d),
            stack(outs_p, 0), stack(outs_p, 1), stack(outs_p, 2), stack(outs_p, 3), stack(outs_p, 4),
            stack(outs_s, 0), stack(outs_s, 1), c_all, stack(outs_s, 3), stack(outs_s, 4))
```

```python
import functools
import math

import jax
import jax.numpy as jnp
from jax import lax
from jax.experimental import pallas as pl
from jax.experimental.pallas import tpu as pltpu

F32 = jnp.float32
BF16 = jnp.bfloat16
EPS = 1e-6
ROPE_THETA = 10000.0
LANE = 128
VMEM_LIMIT_BYTES = 56 * 2**20
_NT = (((1,), (1,)), ((), ()))


def _cparams(*sem):
    return pltpu.CompilerParams(dimension_semantics=sem, vmem_limit_bytes=VMEM_LIMIT_BYTES)


def _rms(x, g):
    return x * lax.rsqrt(jnp.mean(x * x, axis=-1, keepdims=True) + EPS) * g


def _log_sigmoid(x):
    return jnp.minimum(x, 0.0) - jnp.log(1.0 + jnp.exp(-jnp.abs(x)))


def _bdot(a, b):
    return jnp.dot(a, b, preferred_element_type=F32)


def _tile(n, pref):
    if n <= pref:
        return n
    t = pref
    while n % t:
        t -= 8
    return t


def _ffn_body(x_ref, g_ref, wa_ref, wb_ref, wd_ref, *rest, with_ple):
    o_ref, xn_ref, acc_ref = rest[-3:]
    j = pl.program_id(1)

    @pl.when(j == 0)
    def _():
        xn_ref[...] = _rms(x_ref[...], g_ref[...]).astype(BF16)
        acc_ref[...] = jnp.zeros_like(acc_ref)

    xn = xn_ref[...]
    a = _bdot(xn, wa_ref[...])
    b = _bdot(xn, wb_ref[...])
    h = (a * jax.nn.sigmoid(a) * b).astype(BF16)
    acc_ref[...] += _bdot(h, wd_ref[...])

    @pl.when(j == pl.num_programs(1) - 1)
    def _():
        y = x_ref[...] + 0.5 * acc_ref[...]
        if with_ple:
            g2_ref, wpg_ref, p_ref, wple_ref = rest[:4]
            gate = jax.nn.sigmoid(_bdot(_rms(y, g2_ref[...]).astype(BF16), wpg_ref[...]))
            y = y + gate * _bdot(p_ref[...].astype(BF16), wple_ref[...])
        o_ref[...] = y


def _norm_spec(d, layer, k):
    return pl.BlockSpec((None, None, 1, d), lambda i, j: (layer, k, 0, 0))


def ffn(x, gains, w_up, w_down, layer, idx, ple=None, *, tm_pref=512, tf_pref=512):
    m, d = x.shape
    fp = w_down.shape[2]
    tm, tf = _tile(m, tm_pref), _tile(fp, tf_pref)
    in_specs = [
        pl.BlockSpec((tm, d), lambda i, j: (i, 0)),
        _norm_spec(d, layer, 2 * idx),
        pl.BlockSpec((None, None, d, tf), lambda i, j: (layer, idx, 0, j)),
        pl.BlockSpec((None, None, d, tf), lambda i, j: (layer, idx, 0, j)),
        pl.BlockSpec((None, None, tf, d), lambda i, j: (layer, idx, j, 0)),
    ]
    args = [x, gains, w_up[0], w_up[1], w_down]
    if ple is not None:
        wpg, p, wple = ple
        pd = p.shape[-1]
        in_specs += [
            _norm_spec(d, layer, 3),
            pl.BlockSpec((None, d, d), lambda i, j: (layer, 0, 0), pipeline_mode=pl.Buffered(1)),
            pl.BlockSpec((None, tm, pd), lambda i, j: (layer, i, 0)),
            pl.BlockSpec((None, pd, d), lambda i, j: (layer, 0, 0), pipeline_mode=pl.Buffered(1)),
        ]
        args += [gains, wpg, p, wple]
    return pl.pallas_call(
        functools.partial(_ffn_body, with_ple=ple is not None),
        grid=(m // tm, fp // tf),
        in_specs=in_specs,
        out_specs=pl.BlockSpec((tm, d), lambda i, j: (i, 0)),
        out_shape=jax.ShapeDtypeStruct((m, d), F32),
        scratch_shapes=[pltpu.VMEM((tm, d), BF16), pltpu.VMEM((tm, d), F32)],
        compiler_params=_cparams("parallel", "arbitrary"),
        name="ffn_ple" if ple is not None else "ffn",
    )(*args)


def _in_proj_body(x_ref, g_ref, w_ref, o_ref, xn_ref):
    @pl.when(pl.program_id(1) == 0)
    def _():
        xn_ref[...] = _rms(x_ref[...], g_ref[...]).astype(BF16)

    o_ref[...] = _bdot(xn_ref[...], w_ref[...])


def in_proj(x, gains, w, layer, *, tm_pref=512, tn_pref=1536):
    m, d = x.shape
    n = w.shape[-1]
    tm, tn = _tile(m, tm_pref), _tile(n, tn_pref)
    return pl.pallas_call(
        _in_proj_body,
        grid=(m // tm, n // tn),
        in_specs=[
            pl.BlockSpec((tm, d), lambda i, j: (i, 0)),
            _norm_spec(d, layer, 1),
            pl.BlockSpec((None, d, tn), lambda i, j: (layer, 0, j)),
        ],
        out_specs=pl.BlockSpec((tm, tn), lambda i, j: (i, j)),
        out_shape=jax.ShapeDtypeStruct((m, n), F32),
        scratch_shapes=[pltpu.VMEM((tm, d), BF16)],
        compiler_params=_cparams("parallel", "arbitrary"),
        name="in_proj",
    )(x, gains, w)


def _merge_body(x_ref, g_ref, wgm_ref, wga_ref, bm_ref, ba_ref, ym_ref, ya_ref, wo_ref,
                o_ref, xn_ref, mix_ref, *, tn):
    j = pl.program_id(1)

    @pl.when(j == 0)
    def _():
        xn_ref[...] = _rms(x_ref[...], g_ref[...]).astype(BF16)

    xn = xn_ref[...]
    gm = jax.nn.sigmoid(_bdot(xn, wgm_ref[...]) + bm_ref[...])
    ga = jax.nn.sigmoid(_bdot(xn, wga_ref[...]) + ba_ref[...])
    col = pl.multiple_of(j * tn, LANE)
    mix_ref[:, pl.ds(col, tn)] = (gm * ym_ref[...] + ga * ya_ref[...]).astype(BF16)

    @pl.when(j == pl.num_programs(1) - 1)
    def _():
        o_ref[...] = x_ref[...] + _bdot(mix_ref[...], wo_ref[...])


def merge(x, gains, w_gate, b_gate, ym, ya, w_out, layer, *, tm_pref=512, tn_pref=512):
    m, d = x.shape
    tm, tn = _tile(m, tm_pref), _tile(d, tn_pref)
    nj = d // tn
    return pl.pallas_call(
        functools.partial(_merge_body, tn=tn),
        grid=(m // tm, nj),
        in_specs=[
            pl.BlockSpec((tm, d), lambda i, j: (i, 0)),
            _norm_spec(d, layer, 1),
            pl.BlockSpec((None, d, tn), lambda i, j: (layer, 0, j)),
            pl.BlockSpec((None, d, tn), lambda i, j: (layer, 0, nj + j)),
            pl.BlockSpec((None, 1, tn), lambda i, j: (layer, 0, j)),
            pl.BlockSpec((None, 1, tn), lambda i, j: (layer, 0, nj + j)),
            pl.BlockSpec((tm, tn), lambda i, j: (i, j)),
            pl.BlockSpec((tm, tn), lambda i, j: (i, j)),
            pl.BlockSpec((None, d, d), lambda i, j: (layer, 0, 0), pipeline_mode=pl.Buffered(1)),
        ],
        out_specs=pl.BlockSpec((tm, d), lambda i, j: (i, 0)),
        out_shape=jax.ShapeDtypeStruct((m, d), F32),
        scratch_shapes=[pltpu.VMEM((tm, d), BF16), pltpu.VMEM((tm, d), BF16)],
        compiler_params=_cparams("parallel", "arbitrary"),
        name="merge",
    )(x, gains, w_gate, w_gate, b_gate, b_gate, ym, ya, w_out)


def _mlstm_chunk_body(q_ref, k_ref, v_ref, zo_ref, gc_ref, gr_ref, bc_ref, br_ref, gmh_ref,
                      ym_ref, c_out, n_out, m_out, c_scr, n_scr, m_scr, *, nh, dqk, dv, cl):
    c = pl.program_id(1)

    @pl.when(c == 0)
    def _():
        c_scr[...] = jnp.zeros_like(c_scr)
        n_scr[...] = jnp.zeros_like(n_scr)
        m_scr[...] = jnp.zeros_like(m_scr)

    gc = gc_ref[0] + bc_ref[...]
    gr = gr_ref[0, 0] + br_ref[...]
    t_idx = lax.broadcasted_iota(jnp.int32, (cl, cl), 0)
    j_idx = lax.broadcasted_iota(jnp.int32, (cl, cl), 1)
    causal = j_idx <= t_idx
    for h in range(nh):
        i_col = gc[:, h:h + 1]
        f_col = _log_sigmoid(gc[:, nh + h:nh + h + 1])
        i_row = gr[h:h + 1, :]
        f_row = _log_sigmoid(gr[nh + h:nh + h + 1, :])
        b_col = jnp.sum(jnp.where(causal, f_row, 0.0), axis=1, keepdims=True)
        b_row = jnp.sum(jnp.where(t_idx <= j_idx, f_col, 0.0), axis=0, keepdims=True)
        m_prev = m_scr[h:h + 1, 0:1]
        d = jnp.where(causal, b_col - b_row + i_row, -jnp.inf)
        inter = b_col + m_prev
        m_t = jnp.maximum(inter, jnp.max(d, axis=1, keepdims=True))
        w_inter = jnp.exp(inter - m_t)
        qh = q_ref[0, :, h * dqk:(h + 1) * dqk]
        kh = k_ref[0, :, h * dqk:(h + 1) * dqk]
        vb = v_ref[0, :, h * dv:(h + 1) * dv].astype(BF16)
        qb = qh.astype(BF16)
        s = lax.dot_general(qb, kh.astype(BF16), _NT, preferred_element_type=F32) * jnp.exp(d - m_t)
        c_old = c_scr[h]
        n_old = n_scr[h:h + 1, :]
        num = w_inter * _bdot(qb, c_old.astype(BF16)) + _bdot(s.astype(BF16), vb)
        den = w_inter * jnp.sum(qh * n_old, axis=1, keepdims=True) + jnp.sum(s, axis=1, keepdims=True)
        hh = num / jnp.maximum(jnp.abs(den), jnp.exp(-m_t))
        hn = _rms(hh, gmh_ref[:, h * dv:(h + 1) * dv])
        ym_ref[0, :, h * dv:(h + 1) * dv] = jax.nn.sigmoid(zo_ref[0, :, h * dv:(h + 1) * dv]) * hn
        b_last = b_col[cl - 1:cl, :]
        g_col = b_last - b_col + i_col
        g_row = b_last - b_row + i_row
        m_new = jnp.maximum(b_last + m_prev, jnp.max(g_row, axis=1, keepdims=True))
        a_prev = jnp.exp(b_last + m_prev - m_new)
        ka = kh * jnp.exp(g_col - m_new)
        c_scr[h] = a_prev * c_old + _bdot(ka.T.astype(BF16), vb)
        n_scr[h:h + 1, :] = a_prev * n_old + jnp.sum(ka, axis=0, keepdims=True)
        m_scr[h:h + 1, :] = jnp.broadcast_to(m_new, (1, LANE))

    @pl.when(c == pl.num_programs(1) - 1)
    def _():
        c_out[0] = c_scr[...]
        n_out[0] = n_scr[...]
        m_out[0] = m_scr[...]


def mlstm_prompt(z3, gc, gr, bc, br, gmh, *, nh, dqk, dv, cl):
    bsz, s, _ = z3.shape
    nc = s // cl
    wq, wv = nh * dqk, nh * dv
    assert wq % LANE == 0 and wv % wq == 0
    kq, kv = wq // wq, wv // wq
    v_blk = (2 * wq) // wv
    assert (2 * wq) % wv == 0
    body = functools.partial(_mlstm_chunk_body, nh=nh, dqk=dqk, dv=dv, cl=cl)
    return pl.pallas_call(
        body,
        grid=(bsz, nc),
        in_specs=[
            pl.BlockSpec((1, cl, wq), lambda b, c: (b, c, 0)),
            pl.BlockSpec((1, cl, wq), lambda b, c: (b, c, kq)),
            pl.BlockSpec((1, cl, wv), lambda b, c: (b, c, v_blk)),
            pl.BlockSpec((1, cl, wv), lambda b, c: (b, c, v_blk + 1)),
            pl.BlockSpec((1, cl, 2 * nh), lambda b, c: (b, c, 0)),
            pl.BlockSpec((1, 1, 2 * nh, cl), lambda b, c: (b, c, 0, 0)),
            pl.BlockSpec((1, 2 * nh), lambda b, c: (0, 0)),
            pl.BlockSpec((2 * nh, 1), lambda b, c: (0, 0)),
            pl.BlockSpec((1, wv), lambda b, c: (0, 0)),
        ],
        out_specs=[
            pl.BlockSpec((1, cl, wv), lambda b, c: (b, c, 0)),
            pl.BlockSpec((1, nh, dqk, dv), lambda b, c: (b, 0, 0, 0)),
            pl.BlockSpec((1, nh, dqk), lambda b, c: (b, 0, 0)),
            pl.BlockSpec((1, nh, LANE), lambda b, c: (b, 0, 0)),
        ],
        out_shape=[
            jax.ShapeDtypeStruct((bsz, s, wv), F32),
            jax.ShapeDtypeStruct((bsz, nh, dqk, dv), F32),
            jax.ShapeDtypeStruct((bsz, nh, dqk), F32),
            jax.ShapeDtypeStruct((bsz, nh, LANE), F32),
        ],
        scratch_shapes=[pltpu.VMEM((nh, dqk, dv), F32), pltpu.VMEM((nh, dqk), F32),
                        pltpu.VMEM((nh, LANE), F32)],
        compiler_params=_cparams("parallel", "arbitrary"),
        name="mlstm_prompt",
    )(z3, z3, z3, z3, gc, gr, bc, br, gmh)


def _mlstm_step_body(q_ref, k_ref, v_ref, zo_ref, gs_ref, bias_ref, gmh_ref, c_ref, n_ref, m_ref,
                     ym_ref, c_out, n_out, m_out, *, nh, dqk, dv):
    c_ref, c_out = c_ref.at[0], c_out.at[0]
    gates = gs_ref[0] + bias_ref[...]
    eye = (lax.broadcasted_iota(jnp.int32, (dqk, dqk), 0)
           == lax.broadcasted_iota(jnp.int32, (dqk, dqk), 1))

    def as_column(row):
        return jnp.sum(jnp.where(eye, row, 0.0), axis=1, keepdims=True)

    for h in range(nh):
        i_pre = gates[:, h:h + 1]
        logf = _log_sigmoid(gates[:, nh + h:nh + h + 1])
        m_prev = m_ref[0, :, h:h + 1]
        inter = logf + m_prev
        m_t = jnp.maximum(inter, i_pre)
        a_prev = jnp.exp(inter - m_t)
        a_new = jnp.exp(i_pre - m_t)
        qh = q_ref[0, :, h * dqk:(h + 1) * dqk]
        kh = k_ref[0, :, h * dqk:(h + 1) * dqk] * a_new
        vh = v_ref[0, :, h * dv:(h + 1) * dv]
        c_new = a_prev * c_ref[0, h] + as_column(kh) * vh
        n_new = a_prev * n_ref[0, h:h + 1, :] + kh
        num = jnp.sum(as_column(qh) * c_new, axis=0, keepdims=True)
        den = jnp.sum(qh * n_new, axis=1, keepdims=True)
        hh = num / jnp.maximum(jnp.abs(den), jnp.exp(-m_t))
        hn = _rms(hh, gmh_ref[:, h * dv:(h + 1) * dv])
        ym_ref[0, :, h * dv:(h + 1) * dv] = jax.nn.sigmoid(zo_ref[0, :, h * dv:(h + 1) * dv]) * hn
        c_out[0, h] = c_new
        n_out[0, h:h + 1, :] = n_new
        m_out[0, :, h:h + 1] = m_t


def mlstm_step(z3, gs, bias, gmh, state_c, n0, m0, layer, *, nh, dqk, dv):
    db = z3.shape[0]
    wq, wv = nh * dqk, nh * dv
    v_blk = (2 * wq) // wv
    body = functools.partial(_mlstm_step_body, nh=nh, dqk=dqk, dv=dv)
    m3 = m0.reshape(db, 1, nh)
    c_spec = pl.BlockSpec((1, 1, nh, dqk, dv), lambda s: (layer, s, 0, 0, 0))
    in_specs = [
        pl.BlockSpec((1, 1, wq), lambda s: (s, 0, 0)),
        pl.BlockSpec((1, 1, wq), lambda s: (s, 0, 1)),
        pl.BlockSpec((1, 1, wv), lambda s: (s, 0, v_blk)),
        pl.BlockSpec((1, 1, wv), lambda s: (s, 0, v_blk + 1)),
        pl.BlockSpec((1, 1, 2 * nh), lambda s: (s, 0, 0)),
        pl.BlockSpec((1, 2 * nh), lambda s: (0, 0)),
        pl.BlockSpec((1, wv), lambda s: (0, 0)),
        c_spec,
        pl.BlockSpec((1, nh, dqk), lambda s: (s, 0, 0)),
        pl.BlockSpec((1, 1, nh), lambda s: (s, 0, 0)),
    ]
    args = [z3, z3, z3, z3, gs, bias, gmh, state_c, n0, m3]
    return pl.pallas_call(
        body,
        grid=(db,),
        in_specs=in_specs,
        out_specs=[
            pl.BlockSpec((1, 1, wv), lambda s: (s, 0, 0)),
            pl.BlockSpec((1, 1, nh, dqk, dv), lambda s: (0, s, 0, 0, 0)),
            pl.BlockSpec((1, nh, dqk), lambda s: (s, 0, 0)),
            pl.BlockSpec((1, 1, nh), lambda s: (s, 0, 0)),
        ],
        out_shape=[
            jax.ShapeDtypeStruct((db, 1, wv), F32),
            jax.ShapeDtypeStruct((1,) + state_c.shape[1:], F32),
            jax.ShapeDtypeStruct((db, nh, dqk), F32),
            jax.ShapeDtypeStruct((db, 1, nh), F32),
        ],
        compiler_params=_cparams("parallel"),
        name="mlstm_step",
    )(*args)


def _rope_half(xb, cs):
    t = xb * cs
    return t + pltpu.roll(t, 64, axis=1)


def _query_heads(cq_b, wq_ref, cs, gq_ref, h, nope, qk_head, low):
    w = 2 * LANE
    qf = _bdot(cq_b, wq_ref[:, h * w:(h + 1) * w])
    qn = qf[:, :nope]
    qp = jnp.where(low, _rope_half(qf[:, nope:], cs), 0.0)
    ssq = jnp.sum(qn * qn, axis=1, keepdims=True) + jnp.sum(qp * qp, axis=1, keepdims=True)
    r = lax.rsqrt(ssq / qk_head + EPS) * (qk_head ** -0.5)
    return qn * r * gq_ref[:, :nope], qp * r * gq_ref[:, nope:]


def _key_heads(ckv_b, wkv_ref, kpe, kpe_sq, gk_ref, h, nope, qk_head):
    w = 2 * LANE
    kvf = _bdot(ckv_b, wkv_ref[:, h * w:(h + 1) * w])
    kn = kvf[:, :nope]
    r = lax.rsqrt((jnp.sum(kn * kn, axis=1, keepdims=True) + kpe_sq) / qk_head + EPS)
    return kn * r * gk_ref[:, :nope], kpe * r * gk_ref[:, nope:], kvf[:, nope:]


def _mla_prep_body(zcq_ref, zckv_ref, zkr_ref, cs_ref, gcq_ref, gckv_ref, gq_ref, gk_ref,
                   wq_ref, wkv_ref, q_out, k_out, v_out, ckv_out, kpe_out, *, nheads, nope, rope):
    qk_head = nope + rope
    cs = cs_ref[...]
    tm = cs.shape[0]
    low = lax.broadcasted_iota(jnp.int32, (tm, LANE), 1) < rope
    cq_b = _rms(zcq_ref[0], gcq_ref[...]).astype(BF16)
    ckv = _rms(zckv_ref[0], gckv_ref[...])
    ckv_out[0] = ckv
    ckv_b = ckv.astype(BF16)
    kpe = jnp.where(low, _rope_half(zkr_ref[0], cs), 0.0)
    kpe_out[0] = kpe[:, :rope]
    kpe_sq = jnp.sum(kpe * kpe, axis=1, keepdims=True)
    for h in range(nheads):
        qn, qp = _query_heads(cq_b, wq_ref, cs, gq_ref, h, nope, qk_head, low)
        q_out[0, h, :, :nope] = qn.astype(BF16)
        q_out[0, h, :, nope:] = qp.astype(BF16)
        kn, kp, v = _key_heads(ckv_b, wkv_ref, kpe, kpe_sq, gk_ref, h, nope, qk_head)
        k_out[0, h, :, :nope] = kn.astype(BF16)
        k_out[0, h, :, nope:] = kp.astype(BF16)
        v_out[0, h, :, :nope] = v.astype(BF16)
        v_out[0, h, :, nope:] = jnp.ones_like(v, BF16)


def mla_prep(z3, cs, gcq, gckv, gq, gk, wq, wkv, *, col_cq, nheads, nope, rope, av, tm_pref=256):
    bsz, s, _ = z3.shape
    ql, kvl = wq.shape[0], wkv.shape[0]
    assert nope == LANE and 2 * rope == LANE and av == LANE and ql == kvl
    tm = _tile(s, tm_pref)
    cb = col_cq // ql
    kr_blk = (col_cq + ql + kvl) // LANE
    body = functools.partial(_mla_prep_body, nheads=nheads, nope=nope, rope=rope)
    hw = 2 * LANE
    return pl.pallas_call(
        body,
        grid=(bsz, s // tm),
        in_specs=[
            pl.BlockSpec((1, tm, ql), lambda b, i: (b, i, cb)),
            pl.BlockSpec((1, tm, kvl), lambda b, i: (b, i, cb + 1)),
            pl.BlockSpec((1, tm, LANE), lambda b, i: (b, i, kr_blk)),
            pl.BlockSpec((tm, LANE), lambda b, i: (i, 0)),
            pl.BlockSpec((1, ql), lambda b, i: (0, 0)),
            pl.BlockSpec((1, kvl), lambda b, i: (0, 0)),
            pl.BlockSpec((1, hw), lambda b, i: (0, 0)),
            pl.BlockSpec((1, hw), lambda b, i: (0, 0)),
            pl.BlockSpec((ql, nheads * hw), lambda b, i: (0, 0)),
            pl.BlockSpec((kvl, nheads * hw), lambda b, i: (0, 0)),
        ],
        out_specs=[
            pl.BlockSpec((1, nheads, tm, hw), lambda b, i: (b, 0, i, 0)),
            pl.BlockSpec((1, nheads, tm, hw), lambda b, i: (b, 0, i, 0)),
            pl.BlockSpec((1, nheads, tm, 2 * av), lambda b, i: (b, 0, i, 0)),
            pl.BlockSpec((1, tm, kvl), lambda b, i: (b, i, 0)),
            pl.BlockSpec((1, tm, rope), lambda b, i: (b, i, 0)),
        ],
        out_shape=[
            jax.ShapeDtypeStruct((bsz, nheads, s, hw), BF16),
            jax.ShapeDtypeStruct((bsz, nheads, s, hw), BF16),
            jax.ShapeDtypeStruct((bsz, nheads, s, 2 * av), BF16),
            jax.ShapeDtypeStruct((bsz, s, kvl), F32),
            jax.ShapeDtypeStruct((bsz, s, rope), F32),
        ],
        compiler_params=_cparams("parallel", "parallel"),
        name="mla_prep",
    )(z3, z3, z3, cs, gcq, gckv, gq, gk, wq, wkv)


def _flash_body(q_ref, k_ref, v_ref, o_ref, s_scr, m_scr, acc_scr, *, tq, tk, av):
    qi = pl.program_id(2)
    m_scr[...] = jnp.full_like(m_scr, -jnp.inf)
    acc_scr[...] = jnp.zeros_like(acc_scr)
    q = q_ref[0, 0]

    def scores(ki, slot, r0=0):
        off = pl.multiple_of(ki * tk, tk)
        s_scr[slot, r0:, :] = lax.dot_general(q[r0:], k_ref[0, 0, pl.ds(off, tk), :], _NT,
                                              preferred_element_type=F32)

    def update(ki, slot, r0=0, r1=tq, triangular=False):
        off = pl.multiple_of(ki * tk, tk)
        s = s_scr[slot, r0:r1, :]
        if triangular:
            row = lax.broadcasted_iota(jnp.int32, (tk, tk), 0)
            col = lax.broadcasted_iota(jnp.int32, (tk, tk), 1)
            s = jnp.where(col <= row, s, -jnp.inf)
        m_old = m_scr[r0:r1]
        m_new = jnp.maximum(m_old, jnp.max(s, axis=1, keepdims=True))
        alpha = jnp.exp(m_old - m_new)
        p = jnp.exp(s - m_new)
        acc_scr[r0:r1] = alpha * acc_scr[r0:r1] + _bdot(p.astype(BF16), v_ref[0, 0, pl.ds(off, tk), :])
        m_scr[r0:r1] = m_new

    scores(0, 0)

    def pair(i, carry):
        scores(2 * i + 1, 1)
        update(2 * i, 0)
        scores(2 * i + 2, 0)
        update(2 * i + 1, 1)
        return carry

    lax.fori_loop(0, qi, pair, 0)
    scores(2 * qi + 1, 1, tk)
    update(2 * qi, 0, 0, tk, triangular=True)
    update(2 * qi, 0, tk, tq)
    update(2 * qi + 1, 1, tk, tq, triangular=True)
    o_ref[0] = acc_scr[:, :av] / acc_scr[:, av:]


def flash_attention(q, k, v, *, tq_pref=1024):
    bsz, nheads, s, hw = q.shape
    av = v.shape[-1] // 2
    tq = _tile(s, tq_pref)
    tk = tq // 2
    body = functools.partial(_flash_body, tq=tq, tk=tk, av=av)
    return pl.pallas_call(
        body,
        grid=(bsz, nheads, s // tq),
        in_specs=[
            pl.BlockSpec((1, 1, tq, hw), lambda b, h, i: (b, h, i, 0)),
            pl.BlockSpec((1, 1, s, hw), lambda b, h, i: (b, h, 0, 0)),
            pl.BlockSpec((1, 1, s, 2 * av), lambda b, h, i: (b, h, 0, 0)),
        ],
        out_specs=pl.BlockSpec((1, tq, av), lambda b, h, i: (b, i, h)),
        out_shape=jax.ShapeDtypeStruct((bsz, s, nheads * av), F32),
        scratch_shapes=[pltpu.VMEM((2, tq, tk), F32), pltpu.VMEM((tq, 1), F32),
                        pltpu.VMEM((tq, 2 * av), F32)],
        compiler_params=_cparams("parallel", "parallel", "arbitrary"),
        name="flash_attention",
    )(q, k, v)


def _sample_prep_body(zcq_ref, zckv_ref, zkr_ref, cs_ref, gcq_ref, gckv_ref, gq_ref, gk_ref,
                      wq_ref, wkv_ref, wukt_ref, qa_out, qp_out, self_out, v_out, ckv_out, kpe_out,
                      *, nheads, nope, rope):
    qk_head = nope + rope
    cs = cs_ref[...]
    tm = cs.shape[0]
    low = lax.broadcasted_iota(jnp.int32, (tm, LANE), 1) < rope
    cq_b = _rms(zcq_ref[...], gcq_ref[...]).astype(BF16)
    ckv = _rms(zckv_ref[...], gckv_ref[...])
    ckv_out[...] = ckv
    ckv_b = ckv.astype(BF16)
    kpe = jnp.where(low, _rope_half(zkr_ref[...], cs), 0.0)
    kpe_out[...] = kpe[:, :rope]
    kpe_sq = jnp.sum(kpe * kpe, axis=1, keepdims=True)
    for h in range(nheads):
        qn, qp = _query_heads(cq_b, wq_ref, cs, gq_ref, h, nope, qk_head, low)
        kn, kp, v = _key_heads(ckv_b, wkv_ref, kpe, kpe_sq, gk_ref, h, nope, qk_head)
        self_out[h] = jnp.broadcast_to(
            jnp.sum(qn * kn, axis=1, keepdims=True) + jnp.sum(qp * kp, axis=1, keepdims=True), (tm, LANE))
        v_out[h] = v
        qg = (qn * gk_ref[:, :nope]).astype(BF16)
        qa_out[h] = _bdot(qg, wukt_ref[h * nope:(h + 1) * nope, :]).astype(BF16)
        qp_out[h] = (qp * gk_ref[:, nope:])[:, :rope].astype(BF16)


def sample_prep(z, cs, gcq, gckv, gq, gk, wq, wkv, wukt, *, col_cq, nheads, nope, rope, av):
    db = z.shape[0]
    ql, kvl = wq.shape[0], wkv.shape[0]
    cb = col_cq // ql
    kr_blk = (col_cq + ql + kvl) // LANE
    hw = 2 * LANE
    body = functools.partial(_sample_prep_body, nheads=nheads, nope=nope, rope=rope)
    full = lambda shape: pl.BlockSpec(shape, lambda i: (0,) * len(shape))
    return pl.pallas_call(
        body,
        grid=(1,),
        in_specs=[
            pl.BlockSpec((db, ql), lambda i: (0, cb)),
            pl.BlockSpec((db, kvl), lambda i: (0, cb + 1)),
            pl.BlockSpec((db, LANE), lambda i: (0, kr_blk)),
            full((db, LANE)), full((1, ql)), full((1, kvl)), full((1, hw)), full((1, hw)),
            full((ql, nheads * hw)), full((kvl, nheads * hw)), full((nheads * nope, kvl)),
        ],
        out_specs=[
            full((nheads, db, kvl)), full((nheads, db, rope)), full((nheads, db, LANE)),
            full((nheads, db, av)), full((db, kvl)), full((db, rope)),
        ],
        out_shape=[
            jax.ShapeDtypeStruct((nheads, db, kvl), BF16),
            jax.ShapeDtypeStruct((nheads, db, rope), BF16),
            jax.ShapeDtypeStruct((nheads, db, LANE), F32),
            jax.ShapeDtypeStruct((nheads, db, av), F32),
            jax.ShapeDtypeStruct((db, kvl), F32),
            jax.ShapeDtypeStruct((db, rope), F32),
        ],
        compiler_params=_cparams("arbitrary"),
        name="sample_prep",
    )(z, z, z, cs, gcq, gckv, gq, gk, wq, wkv, wukt)


def _decode_body(pt_ref, qa_ref, qp_ref, wukt_ref, *refs, npg, nheads, nope, rope, page, sub):
    ckv_refs = refs[:npg]
    kpe_refs = refs[npg:2 * npg]
    acc_out, m_out, l_out = refs[2 * npg:2 * npg + 3]
    lhs_scr, cb_scr, r_scr, lg_scr, m_scr, l_scr, acc_scr = refs[2 * npg + 3:]
    del pt_ref
    j = pl.program_id(1)
    qk_head = nope + rope
    nk = nheads * nope
    ppt = sub // page

    @pl.when(j == 0)
    def _():
        lhs_scr[0:nk, :] = wukt_ref[...]
        lhs_scr[nk:nk + nheads, :] = qa_ref[0]
        m_scr[...] = jnp.full_like(m_scr, -jnp.inf)
        l_scr[...] = jnp.zeros_like(l_scr)
        acc_scr[...] = jnp.zeros_like(acc_scr)

    qp = qp_ref[0]
    ntrips = npg // ppt

    def project(t):
        pages = range(t * ppt, (t + 1) * ppt)
        cb = jnp.concatenate([ckv_refs[p][0, 0].astype(BF16) for p in pages], axis=0)
        cb_scr[t * sub:(t + 1) * sub, :] = cb
        r_scr[t % 2] = lax.dot_general(lhs_scr[...], cb, _NT, preferred_element_type=F32)

    def logits_of(t):
        pages = range(t * ppt, (t + 1) * ppt)
        kpt = jnp.concatenate([kpe_refs[p][0, 0] for p in pages], axis=1)
        kn = r_scr[t % 2, 0:nk, :].reshape(nheads, nope, sub)
        ssq = jnp.sum(kn * kn, axis=1)
        pe_sq = jnp.sum(kpt * kpt, axis=0, keepdims=True)
        s_pe = _bdot(qp, kpt.astype(BF16))
        lg_scr[:, t * sub:(t + 1) * sub] = (r_scr[t % 2, nk:nk + nheads, :] + s_pe) * lax.rsqrt(
            (ssq + pe_sq) / qk_head + EPS)

    project(0)
    for t in range(ntrips):
        if t + 1 < ntrips:
            project(t + 1)
        logits_of(t)

    logits = lg_scr[...]
    m_old = m_scr[...]
    m_new = jnp.maximum(m_old, jnp.max(logits, axis=1, keepdims=True))
    alpha = jnp.exp(m_old - m_new)
    pr = jnp.exp(logits - m_new)
    l_scr[...] = alpha * l_scr[...] + jnp.sum(pr, axis=1, keepdims=True)
    acc_scr[...] = alpha * acc_scr[...] + _bdot(pr.astype(BF16), cb_scr[...])
    m_scr[...] = m_new

    @pl.when(j == pl.num_programs(1) - 1)
    def _():
        acc_out[0] = acc_scr[...]
        m_out[0] = jnp.broadcast_to(m_scr[...], (nheads, LANE))
        l_out[0] = jnp.broadcast_to(l_scr[...], (nheads, LANE))


def paged_decode(pt_flat, qa, qp, wukt, cache_ckv, cache_kpe_t, layer, *, n_pages, nheads, nope, rope,
                 npg=32, sub=1024):
    db = qa.shape[0]
    page, kvl = cache_ckv.shape[2], cache_ckv.shape[3]
    npg = math.gcd(npg, n_pages)
    sub = math.gcd(sub, npg * page)
    nk = nheads * nope
    body = functools.partial(_decode_body, npg=npg, nheads=nheads, nope=nope, rope=rope, page=page, sub=sub)

    def page_spec(shape, p):
        return pl.BlockSpec((1, 1) + shape, lambda s, j, pt: (layer, pt[s * n_pages + j * npg + p], 0, 0))

    in_specs = [
        pl.BlockSpec((1, nheads, kvl), lambda s, j, pt: (s, 0, 0)),
        pl.BlockSpec((1, nheads, rope), lambda s, j, pt: (s, 0, 0)),
        pl.BlockSpec((nk, kvl), lambda s, j, pt: (0, 0)),
    ]
    in_specs += [page_spec((page, kvl), p) for p in range(npg)]
    in_specs += [page_spec((rope, page), p) for p in range(npg)]
    out_spec = lambda w: pl.BlockSpec((1, nheads, w), lambda s, j, pt: (s, 0, 0))
    grid_spec = pltpu.PrefetchScalarGridSpec(
        num_scalar_prefetch=1,
        grid=(db, n_pages // npg),
        in_specs=in_specs,
        out_specs=[out_spec(kvl), out_spec(LANE), out_spec(LANE)],
        scratch_shapes=[
            pltpu.VMEM((nk + nheads, kvl), BF16),
            pltpu.VMEM((npg * page, kvl), BF16),
            pltpu.VMEM((2, nk + nheads, sub), F32),
            pltpu.VMEM((nheads, npg * page), F32),
            pltpu.VMEM((nheads, 1), F32),
            pltpu.VMEM((nheads, 1), F32),
            pltpu.VMEM((nheads, kvl), F32),
        ],
    )
    return pl.pallas_call(
        body,
        grid_spec=grid_spec,
        out_shape=[
            jax.ShapeDtypeStruct((db, nheads, kvl), F32),
            jax.ShapeDtypeStruct((db, nheads, LANE), F32),
            jax.ShapeDtypeStruct((db, nheads, LANE), F32),
        ],
        compiler_params=_cparams("arbitrary", "arbitrary"),
        name="paged_decode",
    )(pt_flat, qa, qp, wukt, *([cache_ckv] * npg), *([cache_kpe_t] * npg))


def _decode_finish_body(acc_ref, m_ref, l_ref, self_ref, v_ref, wuv_ref, o_ref, *, nheads, av):
    for h in range(nheads):
        m_old = m_ref[h][:, 0:1]
        ls = self_ref[h][:, 0:1]
        m_new = jnp.maximum(m_old, ls)
        alpha = jnp.exp(m_old - m_new)
        ps = jnp.exp(ls - m_new)
        denom = alpha * l_ref[h][:, 0:1] + ps
        ctx = (acc_ref[h] * alpha).astype(BF16)
        o_ref[:, h * av:(h + 1) * av] = (_bdot(ctx, wuv_ref[h]) + ps * v_ref[h]) / denom


def decode_finish(acc, m, l, self_logit, v_self, wuv):
    nheads, db, kvl = acc.shape
    av = wuv.shape[-1]
    full = lambda shape: pl.BlockSpec(shape, lambda i: (0,) * len(shape))
    return pl.pallas_call(
        functools.partial(_decode_finish_body, nheads=nheads, av=av),
        grid=(1,),
        in_specs=[full(acc.shape), full(m.shape), full(l.shape), full(self_logit.shape),
                  full(v_self.shape), full(wuv.shape)],
        out_specs=full((db, nheads * av)),
        out_shape=jax.ShapeDtypeStruct((db, nheads * av), F32),
        compiler_params=_cparams("arbitrary"),
        name="decode_finish",
    )(acc, m, l, self_logit, v_self, wuv)


def _rot_cols(w):
    half = w.shape[-1] // 2
    return jnp.concatenate([-w[..., half:], w[..., :half]], axis=-1)


def _pad_last(w, n):
    return jnp.pad(w, [(0, 0)] * (w.ndim - 1) + [(0, n - w.shape[-1])])


def _stacked_weights(dims, norm_g, w_ffn_up, w_ffn_down, w_in, w_gate, b_gate, w_out, w_ple, w_pg):
    d, dff, fp, nh, dqk, dv = dims["d"], dims["dff"], dims["fp"], dims["nh"], dims["dqk"], dims["dv"]
    ql, kvl, rope = dims["ql"], dims["kvl"], dims["rope"]
    depth = norm_g.shape[0]
    s = {"gains": norm_g.reshape(depth, 4, 1, d)}
    s["w_up"] = (_pad_last(w_ffn_up[..., :dff], fp).astype(BF16), _pad_last(w_ffn_up[..., dff:], fp).astype(BF16))
    s["w_down"] = jnp.pad(w_ffn_down, ((0, 0), (0, 0), (0, fp - dff), (0, 0))).astype(BF16)
    o = [0]
    for width in (nh * dqk, nh * dqk, nh * dv, d, nh, nh, ql, kvl, rope):
        o.append(o[-1] + width)
    zq, zk, zv, zo, zi, zf, zcq, zckv, zkr = (w_in[:, :, o[i]:o[i + 1]] for i in range(9))
    cols = [zq, zk * (dqk ** -0.5), zv, zo, zcq, zckv, zkr, _rot_cols(zkr), zi, zf]
    s["w_in"] = _pad_last(jnp.concatenate(cols, axis=2), dims["nin"]).astype(BF16)
    s["w_gate"] = w_gate.astype(BF16)
    s["b_gate"] = b_gate.reshape(depth, 1, 2 * d)
    s["w_out"] = w_out.astype(BF16)
    s["w_ple"] = w_ple.astype(BF16)
    s["w_pg"] = w_pg.astype(BF16)
    return s


def _layer_weights(l, dims, b_if, g_mh, g_cq, w_uq, g_ckv, w_ukv, g_q, g_k):
    nh, dv = dims["nh"], dims["dv"]
    ah, ql, kvl, nope, rope, av = dims["ah"], dims["ql"], dims["kvl"], dims["nope"], dims["rope"], dims["av"]
    w = {}
    w["b_col"] = b_if[l].reshape(1, 2 * nh)
    w["b_row"] = b_if[l].reshape(2 * nh, 1)
    w["gmh"] = g_mh[l].reshape(1, nh * dv)
    w["gcq"] = g_cq[l].reshape(1, ql)
    w["gckv"] = g_ckv[l].reshape(1, kvl)
    zpad = jnp.zeros((rope,), F32)
    w["gq"] = jnp.concatenate([g_q[l], zpad]).reshape(1, 2 * LANE)
    w["gk"] = jnp.concatenate([g_k[l], zpad]).reshape(1, 2 * LANE)
    uq = w_uq[l]
    pe = uq[:, :, nope:]
    w["wq"] = jnp.concatenate([uq[:, :, :nope], pe, _rot_cols(pe)], axis=-1).reshape(ql, ah * 2 * LANE).astype(BF16)
    ukv = w_ukv[l]
    w["wkv"] = ukv.reshape(kvl, ah * (nope + av)).astype(BF16)
    w["wukt"] = ukv[:, :, :nope].reshape(kvl, ah * nope).T.astype(BF16)
    w["wuv"] = jnp.transpose(ukv[:, :, nope:], (1, 0, 2)).astype(BF16)
    return w


def _rope_table(pos, rope):
    half = rope // 2
    inv = ROPE_THETA ** (-jnp.arange(half, dtype=F32) / half)
    ang = pos.astype(F32)[:, None] * inv[None, :]
    cos, sin = jnp.cos(ang), jnp.sin(ang)
    return jnp.concatenate([cos, cos, sin, sin], axis=1)


def kernel(x_prompt, x_sample, cache_ckv, cache_kpe, state_C, state_n, state_m, page_table,
           p_prompt, p_sample, norm_g, w_ffn_up, w_ffn_down, w_in, b_if, g_mh, g_cq, w_uq,
           g_ckv, w_ukv, g_q, g_k, w_gate, b_gate, w_out, w_ple, w_pg):
    bsz, seq, d = x_prompt.shape
    db, dseq, _ = x_sample.shape
    assert dseq == 1
    depth, _, nh, dqk, dv = state_C.shape
    _, ql, ah, qk_head = w_uq.shape
    kvl = w_ukv.shape[1]
    rope = cache_kpe.shape[-1]
    nope = qk_head - rope
    av = w_ukv.shape[-1] - nope
    dff = w_ffn_down.shape[2]
    page = cache_ckv.shape[2]
    n_pages = page_table.shape[1]
    past = n_pages * page
    pd = p_prompt.shape[-1]
    col_cq = 2 * nh * dqk + nh * dv + d
    n_used = col_cq + ql + kvl + 2 * rope + 2 * nh
    col_gate = col_cq + ql + kvl + 2 * rope
    tn_in = 1536 if d >= 1536 else 256
    dims = dict(d=d, dff=dff, fp=-(-dff // 512) * 512 if dff > 512 else dff, nh=nh, dqk=dqk, dv=dv, ah=ah,
                ql=ql, kvl=kvl, nope=nope, rope=rope, av=av, nin=-(-n_used // tn_in) * tn_in)
    cl = _tile(seq, 256)
    nc = seq // cl

    cs_p = _rope_table(jnp.arange(seq), rope)
    cs_s = jnp.broadcast_to(_rope_table(past + jnp.arange(1), rope), (db, 2 * rope))
    pt_flat = page_table.reshape(-1).astype(jnp.int32)
    cache_kpe_t = jnp.swapaxes(cache_kpe, 2, 3)

    xp = x_prompt.reshape(bsz * seq, d)
    xs = x_sample.reshape(db, d)
    pp = p_prompt.reshape(depth, bsz * seq, pd)
    ps = p_sample.reshape(depth, db, pd)
    outs_p, outs_s = [], []
    sw =_stacked_weights(dims, norm_g, w_ffn_up, w_ffn_down, w_in, w_gate, b_gate, w_out, w_ple, w_pg)
    gains = sw["gains"]
    for l in range(depth):
        w = _layer_weights(l, dims, b_if, g_mh, g_cq, w_uq, g_ckv, w_ukv, g_q, g_k)
        mla = dict(col_cq=col_cq, nheads=ah, nope=nope, rope=rope, av=av)

        xp = ffn(xp, gains, sw["w_up"], sw["w_down"], l, 0)
        z = in_proj(xp, gains, sw["w_in"], l, tn_pref=tn_in)
        z3 = z.reshape(bsz, seq, -1)
        gates = z3[:, :, col_gate:col_gate + 2 * nh]
        gates_r = jnp.swapaxes(gates.reshape(bsz, nc, cl, 2 * nh), 2, 3)
        ym, c_p, n_p, m_p = mlstm_prompt(z3, gates, gates_r, w["b_col"], w["b_row"], w["gmh"],
                                         nh=nh, dqk=dqk, dv=dv, cl=cl)
        qh, kh, vh, ckv_p, kpe_p = mla_prep(z3, cs_p, w["gcq"], w["gckv"], w["gq"], w["gk"],
                                            w["wq"], w["wkv"], **mla)
        ya = flash_attention(qh, kh, vh)
        xp = merge(xp, gains, sw["w_gate"], sw["b_gate"], ym.reshape(bsz * seq, d), ya.reshape(bsz * seq, d),
                   sw["w_out"], l)
        xp = ffn(xp, gains, sw["w_up"], sw["w_down"], l, 1, (sw["w_pg"], pp, sw["w_ple"]))
        outs_p.append((ckv_p, kpe_p, c_p, n_p, m_p[:, :, 0]))

        xs = ffn(xs, gains, sw["w_up"], sw["w_down"], l, 0)
        zs = in_proj(xs, gains, sw["w_in"], l, tn_pref=tn_in)
        zs3 = zs.reshape(db, 1, -1)
        ym_s, c_s, n_s, m_s = mlstm_step(zs3, zs3[:, :, col_gate:col_gate + 2 * nh], w["b_col"], w["gmh"],
                                         state_C, state_n[l], state_m[l], l, nh=nh, dqk=dqk, dv=dv)
        qa, qp, self_logit, v_self, ckv_s, kpe_s = sample_prep(
            zs, cs_s, w["gcq"], w["gckv"], w["gq"], w["gk"], w["wq"], w["wkv"], w["wukt"], **mla)
        acc, m_run, l_run = paged_decode(pt_flat, jnp.swapaxes(qa, 0, 1), jnp.swapaxes(qp, 0, 1), w["wukt"],
                                         cache_ckv, cache_kpe_t, l, n_pages=n_pages, nheads=ah,
                                         nope=nope, rope=rope)
        ya_s = decode_finish(jnp.swapaxes(acc, 0, 1), jnp.swapaxes(m_run, 0, 1), jnp.swapaxes(l_run, 0, 1),
                             self_logit, v_self, w["wuv"])
        xs = merge(xs, gains, sw["w_gate"], sw["b_gate"], ym_s.reshape(db, d), ya_s, sw["w_out"], l)
        xs = ffn(xs, gains, sw["w_up"], sw["w_down"], l, 1, (sw["w_pg"], ps, sw["w_ple"]))
        outs_s.append((ckv_s.reshape(db, 1, kvl), kpe_s.reshape(db, 1, rope), c_s, n_s, m_s.reshape(db, nh)))

    stack = lambda outs, i: jnp.stack([o[i] for o in outs])
    return (xp.reshape(bsz, seq, d), xs.reshape(db, 1, d),
            stack(outs_p, 0), stack(outs_p, 1), stack(outs_p, 2), stack(outs_p, 3), stack(outs_p, 4),
            stack(outs_s, 0), stack(outs_s, 1), jnp.concatenate([o[2] for o in outs_s]), stack(outs_s, 3),
            stack(outs_s, 4))
```
